```python
import math
import jax, jax.numpy as jnp
from jax import lax
import numpy as np

D_MODEL = 4096
BATCH = 2
SEQ = 4096
DEPTH = 2

FOX_HEAD_DIM = 128
FOX_WIDTH = 3 * D_MODEL // 8
FOX_HEADS = FOX_WIDTH // FOX_HEAD_DIM
FOX_BLOCK = 128
MLSTM_HEADS = 4
MLSTM_WIDTH = D_MODEL // 4
MLSTM_V_DIM = MLSTM_WIDTH // MLSTM_HEADS
MLSTM_QK_DIM = MLSTM_V_DIM // 2
MLSTM_QK_WIDTH = MLSTM_HEADS * MLSTM_QK_DIM
MLSTM_CHUNK = 64
MLSTM_CONV = 4
GATE_SOFTCAP = 15.0
RWKV_HEAD_DIM = 64
RWKV_WIDTH = D_MODEL - FOX_WIDTH - MLSTM_WIDTH
RWKV_HEADS = RWKV_WIDTH // RWKV_HEAD_DIM
DECAY_LORA = max(32, int(round(1.8 * D_MODEL ** 0.5 / 32)) * 32)
AAA_LORA = max(32, int(round(1.8 * D_MODEL ** 0.5 / 32)) * 32)
GATE_LORA = max(32, int(round(0.6 * D_MODEL ** 0.8 / 32)) * 32)
RWKV_LN_EPS = 64e-5
D_FF = ((8 * D_MODEL // 3 + 255) // 256) * 256
NORM_EPS = 1e-6

FOX_SIZES = (FOX_WIDTH, FOX_WIDTH, FOX_WIDTH, FOX_HEADS)
MLSTM_SIZES = (2 * MLSTM_QK_WIDTH, MLSTM_WIDTH, MLSTM_HEADS, MLSTM_HEADS, MLSTM_WIDTH)
RWKV_SIZES = (RWKV_WIDTH, RWKV_WIDTH, RWKV_WIDTH, DECAY_LORA, AAA_LORA, GATE_LORA)
FOX_IN = sum(FOX_SIZES)
MLSTM_IN = sum(MLSTM_SIZES)
RWKV_IN = sum(RWKV_SIZES)
GROUP_SIZES = (FOX_IN, MLSTM_IN, RWKV_IN)
N_IN = FOX_IN + MLSTM_IN + RWKV_IN

kernel_name = 'hymba_style_fox_mlstm_rwkv7_hybrid'


def _split(t, sizes):
    idx = np.cumsum(sizes)[:-1].tolist()
    return jnp.split(t, idx, axis=-1)


def _rms(t, g):
    tf = t.astype(jnp.float32)
    y = tf * lax.rsqrt(jnp.mean(tf * tf, axis=-1, keepdims=True) + NORM_EPS)
    return (y * g.astype(jnp.float32)).astype(t.dtype)


def _softcap(t):
    return GATE_SOFTCAP * jnp.tanh(t / GATE_SOFTCAP)


def _causal_conv(t, w):
    K, C = w.shape
    return lax.conv_general_dilated(t, w[:, None, :], window_strides=(1,), padding=[(K - 1, 0)],
                                    dimension_numbers=('NWC', 'WIO', 'NWC'), feature_group_count=C)


def _forgetting_attention(p, f_bias, out_g):
    B, T, _ = p.shape
    q, k, v, f = _split(p, FOX_SIZES)
    q = q.reshape(B, T, FOX_HEADS, FOX_HEAD_DIM)
    k = k.reshape(B, T, FOX_HEADS, FOX_HEAD_DIM)
    v = v.reshape(B, T, FOX_HEADS, FOX_HEAD_DIM)
    logf = jax.nn.log_sigmoid(f.astype(jnp.float32) + f_bias.astype(jnp.float32))
    cumf = jnp.cumsum(logf, axis=1).transpose(0, 2, 1)
    nb = T // FOX_BLOCK
    q_blocks = q.reshape(B, nb, FOX_BLOCK, FOX_HEADS, FOX_HEAD_DIM).transpose(1, 0, 3, 2, 4)
    f_blocks = cumf.reshape(B, FOX_HEADS, nb, FOX_BLOCK).transpose(2, 0, 1, 3)
    key_pos = jnp.arange(T)
    scale = FOX_HEAD_DIM ** -0.5

    def one_block(args):
        qb, fb, bi = args
        s = jnp.einsum('bhqd,bshd->bhqs', qb, k).astype(jnp.float32) * scale
        s = s + fb[..., :, None] - cumf[:, :, None, :]
        q_pos = bi * FOX_BLOCK + jnp.arange(FOX_BLOCK)
        s = jnp.where(key_pos[None, :] <= q_pos[:, None], s, -jnp.inf)
        w = jax.nn.softmax(s, axis=-1).astype(v.dtype)
        return jnp.einsum('bhqs,bshd->bqhd', w, v)

    o = lax.map(one_block, (q_blocks, f_blocks, jnp.arange(nb)))
    o = o.transpose(1, 0, 2, 3, 4).reshape(B, T, FOX_HEADS, FOX_HEAD_DIM)
    return _rms(o, out_g.reshape(FOX_HEADS, FOX_HEAD_DIM)).reshape(B, T, FOX_WIDTH)


def _mlstm(p, conv_w, conv_b, i_bias, f_bias, out_g):
    B, T, _ = p.shape
    f32 = jnp.float32
    qk, v, ig, fg, og = _split(p, MLSTM_SIZES)
    qk = jax.nn.silu(_causal_conv(qk, conv_w) + conv_b)
    q, k = jnp.split(qk, 2, axis=-1)
    nc = T // MLSTM_CHUNK
    L = MLSTM_CHUNK

    def chunks(t, d):
        return t.astype(f32).reshape(B, nc, L, MLSTM_HEADS, d).transpose(1, 0, 3, 2, 4)

    def gchunks(t):
        return t.reshape(B, nc, L, MLSTM_HEADS).transpose(1, 0, 3, 2)

    qc = chunks(q, MLSTM_QK_DIM) * (MLSTM_QK_DIM ** -0.5)
    kc = chunks(k, MLSTM_QK_DIM)
    vc = chunks(v, MLSTM_V_DIM)
    li = gchunks(_softcap(ig.astype(f32) + i_bias.astype(f32)))
    lf = gchunks(jax.nn.log_sigmoid(_softcap(fg.astype(f32) + f_bias.astype(f32))))
    causal = jnp.tril(jnp.ones((L, L), dtype=bool))

    def step(carry, xs):
        Cm, nm, m = carry
        qt, kt, vt, lit, lft = xs
        b = jnp.cumsum(lft, axis=-1)
        g = b[..., -1]
        a_inter = b + m[..., None]
        Dm = jnp.where(causal, b[..., :, None] - b[..., None, :] + lit[..., None, :], -jnp.inf)
        m_t = jnp.maximum(a_inter, jnp.max(Dm, axis=-1))
        w_inter = jnp.exp(a_inter - m_t)
        s = jnp.einsum('bhtd,bhsd->bhts', qt, kt) * jnp.exp(Dm - m_t[..., None])
        num = w_inter[..., None] * jnp.einsum('bhtd,bhde->bhte', qt, Cm) + jnp.einsum('bhts,bhse->bhte', s, vt)
        den = w_inter * jnp.einsum('bhtd,bhd->bht', qt, nm) + jnp.sum(s, axis=-1)
        h = num / jnp.maximum(jnp.abs(den), jnp.exp(-m_t))[..., None]
        upd = g[..., None] - b + lit
        m_new = jnp.maximum(g + m, jnp.max(upd, axis=-1))
        decay = jnp.exp(g + m - m_new)
        wk = jnp.exp(upd - m_new[..., None])
        C_new = decay[..., None, None] * Cm + jnp.einsum('bhs,bhsd,bhse->bhde', wk, kt, vt)
        n_new = decay[..., None] * nm + jnp.einsum('bhs,bhsd->bhd', wk, kt)
        return (C_new, n_new, m_new), h

    init = (jnp.zeros((B, MLSTM_HEADS, MLSTM_QK_DIM, MLSTM_V_DIM), f32),
            jnp.zeros((B, MLSTM_HEADS, MLSTM_QK_DIM), f32),
            jnp.zeros((B, MLSTM_HEADS), f32))
    _, h = lax.scan(step, init, (qc, kc, vc, li, lf))
    h = h.transpose(1, 0, 3, 2, 4).reshape(B, T, MLSTM_HEADS, MLSTM_V_DIM)
    h = _rms(h, out_g.reshape(MLSTM_HEADS, MLSTM_V_DIM)).reshape(B, T, MLSTM_WIDTH).astype(p.dtype)
    return h * jax.nn.sigmoid(og)


def _rwkv7(p, mu, w0, w_up, a0, a_up, g_up, k_k, k_a, r_k, ln_w, ln_b):
    B, T, _ = p.shape
    f32 = jnp.float32
    H, N = RWKV_HEADS, RWKV_HEAD_DIM
    p_prev = jnp.pad(p, ((0, 0), (1, 0), (0, 0)))[:, :-1]
    p = p + (p_prev - p) * mu
    r, k, v, wl, al, gl = _split(p, RWKV_SIZES)
    w = -jax.nn.softplus(-(w0 + jnp.tanh(wl) @ w_up).astype(f32)) - 0.5
    decay = jnp.exp(-jnp.exp(w))
    a = jax.nn.sigmoid((a0 + al @ a_up).astype(f32))
    g = jax.nn.sigmoid(gl) @ g_up

    def heads(t):
        return t.astype(f32).reshape(B, T, H, N)

    r, k, v, a, decay = heads(r), heads(k), heads(v), heads(a), heads(decay)
    kk = k * k_k.astype(f32).reshape(H, N)
    kk = kk / jnp.maximum(jnp.sqrt(jnp.sum(kk * kk, axis=-1, keepdims=True)), 1e-12)
    k = k * (1.0 + (a - 1.0) * k_a.astype(f32).reshape(H, N))

    def step(S, xs):
        r_t, w_t, k_t, v_t, kk_t, a_t = xs
        sk = jnp.einsum('bhij,bhj->bhi', S, kk_t)
        S = S * w_t[:, :, None, :] - sk[..., None] * (kk_t * a_t)[:, :, None, :] + v_t[..., None] * k_t[:, :, None, :]
        return S, jnp.einsum('bhij,bhj->bhi', S, r_t)

    def seq_first(t):
        return t.transpose(1, 0, 2, 3)

    S0 = jnp.zeros((B, H, N, N), f32)
    _, y = lax.scan(step, S0, (seq_first(r), seq_first(decay), seq_first(k), seq_first(v), seq_first(kk), seq_first(a)))
    y = y.transpose(1, 0, 2, 3)
    mean = jnp.mean(y, axis=-1, keepdims=True)
    var = jnp.mean(jnp.square(y - mean), axis=-1, keepdims=True)
    y = (y - mean) * lax.rsqrt(var + RWKV_LN_EPS)
    y = y * ln_w.astype(f32).reshape(H, N) + ln_b.astype(f32).reshape(H, N)
    y = y + jnp.sum(r * k * r_k.astype(f32), axis=-1, keepdims=True) * v
    return y.reshape(B, T, RWKV_WIDTH).astype(p.dtype) * g


def setup_inputs(seed: int = 0) -> dict:
    key = jax.random.key(seed)
    ks = iter(jax.random.split(key, 40))

    def nrm(shape, scale=1.0):
        return jax.random.normal(next(ks), shape, jnp.float32) * scale

    def gain(shape):
        return 1.0 + 0.05 * nrm(shape)

    L, D = DEPTH, D_MODEL
    return {
        'x': nrm((BATCH, SEQ, D)),
        'c': nrm((BATCH, D)),
        'ada_w': nrm((L, D, 6 * D), D ** -0.5),
        'ada_b': nrm((L, 6 * D), 0.02),
        'norm1': gain((L, D)),
        'w_in': nrm((L, D, N_IN), D ** -0.5),
        'fox_f_bias': jnp.linspace(1.0, 6.0, FOX_HEADS)[None, :] + 0.1 * nrm((L, FOX_HEADS)),
        'fox_norm': gain((L, FOX_WIDTH)),
        'ml_conv_w': nrm((L, MLSTM_CONV, 2 * MLSTM_QK_WIDTH), MLSTM_CONV ** -0.5),
        'ml_conv_b': nrm((L, 2 * MLSTM_QK_WIDTH), 0.02),
        'ml_i_bias': -2.0 + 0.1 * nrm((L, MLSTM_HEADS)),
        'ml_f_bias': jnp.linspace(3.0, 6.0, MLSTM_HEADS)[None, :] + 0.1 * nrm((L, MLSTM_HEADS)),
        'ml_norm': gain((L, MLSTM_WIDTH)),
        'rw_mu': jax.random.uniform(next(ks), (L, RWKV_IN), jnp.float32),
        'rw_w0': -3.0 + 0.5 * nrm((L, RWKV_WIDTH)),
        'rw_w_up': nrm((L, DECAY_LORA, RWKV_WIDTH), DECAY_LORA ** -0.5),
        'rw_a0': 0.1 * nrm((L, RWKV_WIDTH)),
        'rw_a_up': nrm((L, AAA_LORA, RWKV_WIDTH), AAA_LORA ** -0.5),
        'rw_g_up': nrm((L, GATE_LORA, RWKV_WIDTH), GATE_LORA ** -0.5),
        'rw_k_k': 1.0 + 0.1 * nrm((L, RWKV_WIDTH)),
        'rw_k_a': 1.0 + 0.1 * nrm((L, RWKV_WIDTH)),
        'rw_r_k': 0.1 * nrm((L, RWKV_HEADS, RWKV_HEAD_DIM)),
        'rw_ln_w': gain((L, RWKV_WIDTH)),
        'rw_ln_b': nrm((L, RWKV_WIDTH), 0.02),
        'w_out': nrm((L, D, D), D ** -0.5),
        'norm2': gain((L, D)),
        'ffn_gate': nrm((L, D, D_FF), D ** -0.5),
        'ffn_up': nrm((L, D, D_FF), D ** -0.5),
        'ffn_down': nrm((L, D_FF, D), D_FF ** -0.5),
        'final_norm': gain((D,)),
    }


def reference(x, c, ada_w, ada_b, norm1, w_in, fox_f_bias, fox_norm, ml_conv_w, ml_conv_b, ml_i_bias,
              ml_f_bias, ml_norm, rw_mu, rw_w0, rw_w_up, rw_a0, rw_a_up, rw_g_up, rw_k_k, rw_k_a, rw_r_k,
              rw_ln_w, rw_ln_b, w_out, norm2, ffn_gate, ffn_up, ffn_down, final_norm):
    for l in range(DEPTH):
        mod = jax.nn.silu(c) @ ada_w[l] + ada_b[l]
        sh1, sc1, g1, sh2, sc2, g2 = jnp.split(mod[:, None, :], 6, axis=-1)
        h = _rms(x, norm1[l]) * (1.0 + sc1) + sh1
        p = h @ w_in[l]
        p_fox, p_ml, p_rw = _split(p, GROUP_SIZES)
        y_fox = _forgetting_attention(p_fox, fox_f_bias[l], fox_norm[l])
        y_ml = _mlstm(p_ml, ml_conv_w[l], ml_conv_b[l], ml_i_bias[l], ml_f_bias[l], ml_norm[l])
        y_rw = _rwkv7(p_rw, rw_mu[l], rw_w0[l], rw_w_up[l], rw_a0[l], rw_a_up[l], rw_g_up[l],
                      rw_k_k[l], rw_k_a[l], rw_r_k[l], rw_ln_w[l], rw_ln_b[l])
        mix = jnp.concatenate([y_fox, y_ml, y_rw], axis=-1) @ w_out[l]
        x = x + g1 * mix
        h = _rms(x, norm2[l]) * (1.0 + sc2) + sh2
        x = x + g2 * ((jax.nn.silu(h @ ffn_gate[l]) * (h @ ffn_up[l])) @ ffn_down[l])
    return _rms(x, final_norm)
```

```python
import functools

import jax
import jax.numpy as jnp
from jax import lax
from jax.experimental import pallas as pl
from jax.experimental.pallas import tpu as pltpu

F32 = jnp.float32
BF16 = jnp.bfloat16
HIGHEST = lax.Precision.HIGHEST

D_MODEL = 4096
FOX_HEADS, FOX_HEAD_DIM = 12, 128
FOX_WIDTH = FOX_HEADS * FOX_HEAD_DIM
ML_HEADS, ML_QK_DIM, ML_V_DIM = 4, 128, 256
ML_QK_WIDTH = ML_HEADS * ML_QK_DIM
ML_WIDTH = ML_HEADS * ML_V_DIM
ML_CONV = 4
GATE_SOFTCAP = 15.0
RW_HEADS, RW_HEAD_DIM = 24, 64
RW_WIDTH = RW_HEADS * RW_HEAD_DIM
RW_LORA = 128
RW_GATE_LORA = 480
RW_GATE_LORA_PAD = 512
RW_LN_EPS = 64e-5
D_FF = 11008
D_FF_PAD = 11264
NORM_EPS = 1e-6

LANE = 128
SUBLANE = 8
VMEM_LIMIT = 56 * 1024 * 1024

_O_FOX_Q, _O_FOX_K, _O_FOX_V, _O_FOX_F = 0, 1536, 3072, 4608
_O_ML = 4620
_O_ML_QK, _O_ML_V, _O_ML_I, _O_ML_F, _O_ML_O = _O_ML, _O_ML + 1024, _O_ML + 2048, _O_ML + 2052, _O_ML + 2056
_O_RW = 7700
_O_RW_R, _O_RW_K, _O_RW_V = _O_RW, _O_RW + 1536, _O_RW + 3072
_O_RW_WL, _O_RW_AL, _O_RW_GL = _O_RW + 4608, _O_RW + 4736, _O_RW + 4864
N_IN = 13044
P_FOX_Q, P_FOX_K, P_FOX_V = 0, 1536, 3072
P_ML_V, P_ML_O, P_ML_Q, P_ML_K = 4608, 5632, 6656, 7168
P_RW_R, P_RW_K, P_RW_V = 7680, 9216, 10752
P_RW_GL, P_RW_WL, P_RW_AL = 12288, 12800, 12928
P_FOX_F, P_ML_IF = 13056, 13184
N_PAD = 13312
_SECTIONS = (
    (P_FOX_Q, _O_FOX_Q, 3 * FOX_WIDTH),
    (P_ML_V, _O_ML_V, ML_WIDTH),
    (P_ML_O, _O_ML_O, ML_WIDTH),
    (P_ML_Q, _O_ML_QK, 2 * ML_QK_WIDTH),
    (P_RW_R, _O_RW_R, 3 * RW_WIDTH),
    (P_RW_GL, _O_RW_GL, RW_GATE_LORA),
    (P_RW_WL, _O_RW_WL, 2 * RW_LORA),
    (P_FOX_F, _O_FOX_F, FOX_HEADS),
    (P_ML_IF, _O_ML_I, 2 * ML_HEADS),
)


def _pad_columns(w):
    parts, pos = [], 0
    for p_start, o_start, width in _SECTIONS:
        if p_start > pos:
            parts.append(jnp.zeros(w.shape[:-1] + (p_start - pos,), w.dtype))
        parts.append(w[..., o_start:o_start + width])
        pos = p_start + width
    if pos < N_PAD:
        parts.append(jnp.zeros(w.shape[:-1] + (N_PAD - pos,), w.dtype))
    return jnp.concatenate(parts, axis=-1)


def _params(*sem):
    return pltpu.CompilerParams(dimension_semantics=sem, vmem_limit_bytes=VMEM_LIMIT)


def _log_sigmoid(x):
    return jnp.minimum(x, 0.0) - jnp.log1p(jnp.exp(-jnp.abs(x)))


def _sigmoid(x):
    return 1.0 / (1.0 + jnp.exp(-x))


def _silu(x):
    return x * _sigmoid(x)


def _dot(a, b, precision=None):
    return jnp.dot(a, b, preferred_element_type=F32, precision=precision)


def _dot_nt(a, b, precision=None):
    return lax.dot_general(a, b, (((1,), (1,)), ((), ())), preferred_element_type=F32, precision=precision)


def _dot_tn(a, b, precision=None):
    return lax.dot_general(a, b, (((0,), (0,)), ((), ())), preferred_element_type=F32, precision=precision)


def _mod_kernel(c_ref, w_ref, b_ref, o_ref):
    a = _silu(c_ref[...]).astype(BF16)
    o_ref[0] = _dot(a, w_ref[0].astype(BF16)) + b_ref[0]


def _modulation(c, ada_w, ada_b):
    depth, d, n = ada_w.shape
    batch = c.shape[0]
    rows = -(-batch // SUBLANE) * SUBLANE
    c_pad = jnp.zeros((rows, d), F32).at[:batch].set(c)
    tn = 512
    out = pl.pallas_call(
        _mod_kernel,
        grid=(depth, n // tn),
        in_specs=[
            pl.BlockSpec((rows, d), lambda l, j: (0, 0)),
            pl.BlockSpec((1, d, tn), lambda l, j: (l, 0, j)),
            pl.BlockSpec((1, 1, tn), lambda l, j: (l, 0, j)),
        ],
        out_specs=pl.BlockSpec((1, rows, tn), lambda l, j: (l, 0, j)),
        out_shape=jax.ShapeDtypeStruct((depth, rows, n), F32),
        compiler_params=_params("parallel", "parallel"),
        name="adaln_modulation",
    )(c_pad, ada_w, ada_b.reshape(depth, 1, n))
    return out[:, :batch]


def _norm_kernel(x_ref, g_ref, sc_ref, sh_ref, o_ref):
    x = x_ref[0]
    y = x * lax.rsqrt(jnp.mean(x * x, axis=-1, keepdims=True) + NORM_EPS)
    y = (y * g_ref[...]) * (1.0 + sc_ref[0]) + sh_ref[0]
    o_ref[0] = y.astype(o_ref.dtype)


def _norm_mod(x, g, scale, shift, out_dtype):
    batch, t, d = x.shape
    tm = min(512, t)
    return pl.pallas_call(
        _norm_kernel,
        grid=(batch, t // tm),
        in_specs=[
            pl.BlockSpec((1, tm, d), lambda b, i: (b, i, 0)),
            pl.BlockSpec((1, d), lambda b, i: (0, 0)),
            pl.BlockSpec((1, 1, d), lambda b, i: (b, 0, 0)),
            pl.BlockSpec((1, 1, d), lambda b, i: (b, 0, 0)),
        ],
        out_specs=pl.BlockSpec((1, tm, d), lambda b, i: (b, i, 0)),
        out_shape=jax.ShapeDtypeStruct((batch, t, d), out_dtype),
        compiler_params=_params("parallel", "parallel"),
        name="rmsnorm_modulate",
    )(x, g.reshape(1, d), scale, shift)


def _mm_plain_kernel(a_ref, w_ref, o_ref):
    o_ref[0] = _dot(a_ref[0], w_ref[...]).astype(o_ref.dtype)


def _mm_swiglu_kernel(a_ref, wg_ref, wu_ref, o_ref):
    a = a_ref[0]
    gate = _dot(a, wg_ref[...])
    up = _dot(a, wu_ref[...])
    o_ref[0] = (_silu(gate) * up).astype(o_ref.dtype)


def _mm_residual_kernel(a_ref, w_ref, x_ref, g_ref, o_ref, acc_ref, *, nk):
    k = pl.program_id(3)

    @pl.when(k == 0)
    def _():
        acc_ref[...] = jnp.zeros_like(acc_ref)

    acc_ref[...] += _dot(a_ref[0], w_ref[...])

    @pl.when(k == nk - 1)
    def _():
        o_ref[0] = x_ref[0] + g_ref[0] * acc_ref[...]


def _matmul_plain(a, w, out_dtype, tm, tn):
    batch, t, kdim = a.shape
    n = w.shape[1]
    tm = min(tm, t)
    return pl.pallas_call(
        _mm_plain_kernel,
        grid=(batch, t // tm, n // tn),
        in_specs=[
            pl.BlockSpec((1, tm, kdim), lambda b, i, j: (b, i, 0)),
            pl.BlockSpec((kdim, tn), lambda b, i, j: (0, j)),
        ],
        out_specs=pl.BlockSpec((1, tm, tn), lambda b, i, j: (b, i, j)),
        out_shape=jax.ShapeDtypeStruct((batch, t, n), out_dtype),
        compiler_params=_params("parallel", "parallel", "arbitrary"),
        name="input_projection",
    )(a, w)


def _matmul_swiglu(a, wg, wu, tm, tn):
    batch, t, kdim = a.shape
    n = wg.shape[1]
    tm = min(tm, t)
    return pl.pallas_call(
        _mm_swiglu_kernel,
        grid=(batch, t // tm, n // tn),
        in_specs=[
            pl.BlockSpec((1, tm, kdim), lambda b, i, j: (b, i, 0)),
            pl.BlockSpec((kdim, tn), lambda b, i, j: (0, j)),
            pl.BlockSpec((kdim, tn), lambda b, i, j: (0, j)),
        ],
        out_specs=pl.BlockSpec((1, tm, tn), lambda b, i, j: (b, i, j)),
        out_shape=jax.ShapeDtypeStruct((batch, t, n), BF16),
        compiler_params=_params("parallel", "parallel", "arbitrary"),
        name="ffn_gate_up_swiglu",
    )(a, wg, wu)


def _matmul_residual(a, w, x, gate, tm, tn, tk, name):
    batch, t, kdim = a.shape
    n = w.shape[1]
    tm = min(tm, t)
    nk = kdim // tk
    return pl.pallas_call(
        functools.partial(_mm_residual_kernel, nk=nk),
        grid=(batch, t // tm, n // tn, nk),
        in_specs=[
            pl.BlockSpec((1, tm, tk), lambda b, i, j, k: (b, i, k)),
            pl.BlockSpec((tk, tn), lambda b, i, j, k: (k, j)),
            pl.BlockSpec((1, tm, tn), lambda b, i, j, k: (b, i, j)),
            pl.BlockSpec((1, 1, tn), lambda b, i, j, k: (b, 0, j)),
        ],
        out_specs=pl.BlockSpec((1, tm, tn), lambda b, i, j, k: (b, i, j)),
        out_shape=jax.ShapeDtypeStruct((batch, t, n), F32),
        scratch_shapes=[pltpu.VMEM((tm, tn), F32)],
        compiler_params=_params("parallel", "parallel", "parallel", "arbitrary"),
        name=name,
    )(a, w, x, gate)


ML_CHUNK = 128
_CUM_ROWS = 256


def _gates_kernel(ff_ref, mif_ref, fb_ref, mb_ref, cum_ref, li_ref, bcs_ref, *, t):
    r = _CUM_ROWS if t % _CUM_ROWS == 0 else t
    row = lax.broadcasted_iota(jnp.int32, (r, r), 0)
    col = lax.broadcasted_iota(jnp.int32, (r, r), 1)
    tril = (col <= row).astype(F32)
    fb = fb_ref[...]

    def cum_body(i, carry):
        start = pl.multiple_of(i * r, r)
        lg = _log_sigmoid(ff_ref[0, pl.ds(start, r), :] + fb)
        cs = _dot(tril, lg, HIGHEST) + carry
        cum_ref[0, pl.ds(start, r), :] = cs
        return cs[r - 1:r, :]

    lax.fori_loop(0, t // r, cum_body, jnp.zeros((1, LANE), F32))

    lc = ML_CHUNK
    rowc = lax.broadcasted_iota(jnp.int32, (lc, lc), 0)
    colc = lax.broadcasted_iota(jnp.int32, (lc, lc), 1)
    trilc = (colc <= rowc).astype(F32)
    mb = mb_ref[...]

    def ml_body(i, carry):
        start = pl.multiple_of(i * lc, lc)
        z = mif_ref[0, pl.ds(start, lc), :] + mb
        capped = GATE_SOFTCAP * jnp.tanh(z / GATE_SOFTCAP)
        li_ref[0, pl.ds(start, lc), :] = capped
        bcs_ref[0, pl.ds(start, lc), :] = _dot(trilc, _log_sigmoid(capped), HIGHEST)
        return carry

    lax.fori_loop(0, t // lc, ml_body, 0)


def _gates(p, fox_bias_row, ml_bias_row):
    batch, t, _ = p.shape
    blk = lambda col: pl.BlockSpec((1, t, LANE), lambda b: (b, 0, col // LANE))
    row = pl.BlockSpec((1, LANE), lambda b: (0, 0))
    out = pl.BlockSpec((1, t, LANE), lambda b: (b, 0, 0))
    shp = jax.ShapeDtypeStruct((batch, t, LANE), F32)
    return pl.pallas_call(
        functools.partial(_gates_kernel, t=t),
        grid=(batch,),
        in_specs=[blk(P_FOX_F), blk(P_ML_IF), row, row],
        out_specs=[out, out, out],
        out_shape=[shp, shp, shp],
        compiler_params=_params("parallel"),
        name="gate_cumsums",
    )(p, p, fox_bias_row, ml_bias_row)


def _fox_kernel(q_ref, k_ref, v_ref, fq_ref, fk_ref, g_ref, o_ref, *, tq):
    i = pl.program_id(2)
    scale = FOX_HEAD_DIM ** -0.5
    q = (q_ref[0] * scale).astype(BF16)
    fq = fq_ref[0, 0]
    q_pos = i * tq + lax.broadcasted_iota(jnp.int32, (tq, 1), 0)

    def body(j, carry):
        m, l, acc = carry
        start = pl.multiple_of(j * tq, tq)
        ks = k_ref[0, pl.ds(start, tq), :].astype(BF16)
        vs = v_ref[0, pl.ds(start, tq), :].astype(BF16)
        fk = fk_ref[0, 0, :, pl.ds(start, tq)]
        s = _dot_nt(q, ks) + (fq - fk)
        k_pos = start + lax.broadcasted_iota(jnp.int32, (1, tq), 1)
        s = jnp.where(k_pos <= q_pos, s, -jnp.inf)
        m_new = jnp.maximum(m, jnp.max(s, axis=-1, keepdims=True))
        p = jnp.exp(s - m_new)
        alpha = jnp.exp(m - m_new)
        l = alpha * l + jnp.sum(p, axis=-1, keepdims=True)
        acc = alpha * acc + _dot(p.astype(BF16), vs)
        return m_new, l, acc

    init = (jnp.full((tq, 1), -1e30, F32), jnp.zeros((tq, 1), F32), jnp.zeros((tq, FOX_HEAD_DIM), F32))
    _, l, acc = lax.fori_loop(0, i + 1, body, init)
    o = acc / l
    o = o * lax.rsqrt(jnp.mean(o * o, axis=-1, keepdims=True) + NORM_EPS)
    o_ref[0] = (o * g_ref[...]).astype(o_ref.dtype)


def _fox_attention(p, cum_col, cum_row, out_g):
    batch, t, _ = p.shape
    tq = min(512, t)
    dh = FOX_HEAD_DIM
    return pl.pallas_call(
        functools.partial(_fox_kernel, tq=tq),
        grid=(batch, FOX_HEADS, t // tq),
        in_specs=[
            pl.BlockSpec((1, tq, dh), lambda b, h, i: (b, i, P_FOX_Q // dh + h)),
            pl.BlockSpec((1, t, dh), lambda b, h, i: (b, 0, P_FOX_K // dh + h)),
            pl.BlockSpec((1, t, dh), lambda b, h, i: (b, 0, P_FOX_V // dh + h)),
            pl.BlockSpec((1, 1, tq, 1), lambda b, h, i: (b, h, i, 0)),
            pl.BlockSpec((1, 1, 1, t), lambda b, h, i: (b, h, 0, 0)),
            pl.BlockSpec((1, dh), lambda b, h, i: (0, h)),
        ],
        out_specs=pl.BlockSpec((1, tq, dh), lambda b, h, i: (b, i, h)),
        out_shape=jax.ShapeDtypeStruct((batch, t, FOX_WIDTH), BF16),
        compiler_params=_params("parallel", "parallel", "arbitrary"),
        name="fox_attention",
    )(p, p, p, cum_col, cum_row, out_g)


def _mlstm_kernel(q_ref, k_ref, v_ref, o_ref, grow_ref, gcol_ref, cwq_ref, cwk_ref, cbq_ref, cbk_ref,
                  ng_ref, y_ref, *, t):
    lc = ML_CHUNK
    dk, dv = ML_QK_DIM, ML_V_DIM
    row = lax.broadcasted_iota(jnp.int32, (lc, lc), 0)
    col = lax.broadcasted_iota(jnp.int32, (lc, lc), 1)
    causal = col <= row
    cwq, cwk = cwq_ref[...], cwk_ref[...]
    cbq, cbk = cbq_ref[...], cbk_ref[...]
    norm_g = ng_ref[...]

    def conv_silu(ref, start, prev_ok, w, b):
        cur = ref[0, pl.ds(start, lc), :]
        prev = ref[0, pl.ds(jnp.maximum(start - SUBLANE, 0), SUBLANE), :]
        prev = jnp.where(prev_ok, prev, 0.0)
        ext = jnp.concatenate([prev, cur], axis=0)
        out = b
        for j in range(ML_CONV):
            off = SUBLANE - (ML_CONV - 1) + j
            out = out + ext[off:off + lc, :] * w[j:j + 1, :]
        return _silu(out)

    def body(c, carry):
        cmat, nvec, m = carry
        start = pl.multiple_of(c * lc, lc)
        prev_ok = c > 0
        qt = conv_silu(q_ref, start, prev_ok, cwq, cbq) * (dk ** -0.5)
        kt = conv_silu(k_ref, start, prev_ok, cwk, cbk)
        vt = v_ref[0, pl.ds(start, lc), :]
        li_row = grow_ref[0, 0, 0:1, pl.ds(start, lc)]
        b_row = grow_ref[0, 0, 1:2, pl.ds(start, lc)]
        li_col = gcol_ref[0, 0, pl.ds(start, lc), 0:1]
        b_col = gcol_ref[0, 0, pl.ds(start, lc), 1:2]
        g = b_row[:, lc - 1:lc]

        a_inter = b_col + m
        dm = jnp.where(causal, b_col - b_row + li_row, -jnp.inf)
        m_t = jnp.maximum(a_inter, jnp.max(dm, axis=-1, keepdims=True))
        w_inter = jnp.exp(a_inter - m_t)
        qb, kb = qt.astype(BF16), kt.astype(BF16)
        s = _dot_nt(qb, kb) * jnp.exp(dm - m_t)
        num = w_inter * _dot(qb, cmat.astype(BF16)) + _dot(s.astype(BF16), vt.astype(BF16))
        den = w_inter * jnp.sum(qt * nvec, axis=-1, keepdims=True) + jnp.sum(s, axis=-1, keepdims=True)
        h = num / jnp.maximum(jnp.abs(den), jnp.exp(-m_t))

        upd = g - b_col + li_col
        m_new = jnp.maximum(g + m, jnp.max(upd, axis=0, keepdims=True))
        decay = jnp.exp(g + m - m_new)
        wk = jnp.exp(upd - m_new) * kt
        cmat = decay * cmat + _dot_tn(wk.astype(BF16), vt.astype(BF16))
        nvec = decay * nvec + jnp.sum(wk, axis=0, keepdims=True)

        hn = h * lax.rsqrt(jnp.mean(h * h, axis=-1, keepdims=True) + NORM_EPS) * norm_g
        og = o_ref[0, pl.ds(start, lc), :]
        y_ref[0, pl.ds(start, lc), :] = (hn * _sigmoid(og)).astype(y_ref.dtype)
        return cmat, nvec, m_new

    init = (jnp.zeros((dk, dv), F32), jnp.zeros((1, dk), F32), jnp.zeros((1, 1), F32))
    lax.fori_loop(0, t // lc, body, init)


def _mlstm(p, gate_rows, gate_cols, conv_w, conv_b, norm_g):
    batch, t, _ = p.shape
    dk, dv = ML_QK_DIM, ML_V_DIM
    nh = ML_HEADS
    return pl.pallas_call(
        functools.partial(_mlstm_kernel, t=t),
        grid=(batch, nh),
        in_specs=[
            pl.BlockSpec((1, t, dk), lambda b, h: (b, 0, P_ML_Q // dk + h)),
            pl.BlockSpec((1, t, dk), lambda b, h: (b, 0, P_ML_K // dk + h)),
            pl.BlockSpec((1, t, dv), lambda b, h: (b, 0, P_ML_V // dv + h)),
            pl.BlockSpec((1, t, dv), lambda b, h: (b, 0, P_ML_O // dv + h)),
            pl.BlockSpec((1, 1, 2, t), lambda b, h: (b, h, 0, 0)),
            pl.BlockSpec((1, 1, t, 2), lambda b, h: (b, h, 0, 0)),
            pl.BlockSpec((ML_CONV, dk), lambda b, h: (0, h)),
            pl.BlockSpec((ML_CONV, dk), lambda b, h: (0, nh + h)),
            pl.BlockSpec((1, dk), lambda b, h: (0, h)),
            pl.BlockSpec((1, dk), lambda b, h: (0, nh + h)),
            pl.BlockSpec((1, dv), lambda b, h: (0, h)),
        ],
        out_specs=pl.BlockSpec((1, t, dv), lambda b, h: (b, 0, h)),
        out_shape=jax.ShapeDtypeStruct((batch, t, ML_WIDTH), BF16),
        compiler_params=_params("parallel", "parallel"),
        name="mlstm_chunkwise",
    )(p, p, p, p, gate_rows, gate_cols, conv_w, conv_w, conv_b, conv_b, norm_g)


def _rw_prep_kernel(r_ref, k_ref, v_ref, wl_ref, al_ref, gl_ref,
                    rp_ref, kp_ref, vp_ref, wlp_ref, alp_ref, glp_ref,
                    mur_ref, muk_ref, muv_ref, muw_ref, mua_ref, mug_ref,
                    w0_ref, a0_ref, kk_ref, ka_ref, wup_ref, aup_ref, gup_ref,
                    ro_ref, ko_ref, vo_ref, kko_ref, bo_ref, lwo_ref, go_ref, *, tm):
    i = pl.program_id(1)
    first = lax.broadcasted_iota(jnp.int32, (tm, 1), 0) == 0
    has_prev = i > 0

    def shift(cur_ref, prev_ref, mu_ref):
        cur = cur_ref[0]
        last = jnp.where(has_prev, prev_ref[0][SUBLANE - 1:SUBLANE, :], 0.0)
        prev = jnp.where(first, last, pltpu.roll(cur, 1, axis=0))
        return cur + (prev - cur) * mu_ref[...]

    r = shift(r_ref, rp_ref, mur_ref)
    k = shift(k_ref, kp_ref, muk_ref)
    v = shift(v_ref, vp_ref, muv_ref)
    wl = shift(wl_ref, wlp_ref, muw_ref)
    al = shift(al_ref, alp_ref, mua_ref)
    gl = shift(gl_ref, glp_ref, mug_ref)

    z = w0_ref[...] + _dot(jnp.tanh(wl), wup_ref[...], HIGHEST)
    log_decay = -jnp.exp(_log_sigmoid(z) - 0.5)
    a = _sigmoid(a0_ref[...] + _dot(al, aup_ref[...], HIGHEST))
    g = _dot(_sigmoid(gl), gup_ref[...], HIGHEST)

    kk = k * kk_ref[...]
    hrow = lax.broadcasted_iota(jnp.int32, (LANE, LANE), 0) // RW_HEAD_DIM
    hcol = lax.broadcasted_iota(jnp.int32, (LANE, LANE), 1) // RW_HEAD_DIM
    same_head = (hrow == hcol).astype(F32)
    ss = _dot(kk * kk, same_head, HIGHEST)
    kk = kk / jnp.maximum(jnp.sqrt(ss), 1e-12)

    ro_ref[0] = r
    ko_ref[0] = k * (1.0 + (a - 1.0) * ka_ref[...])
    vo_ref[0] = v
    kko_ref[0] = kk
    bo_ref[0] = kk * a
    lwo_ref[0] = log_decay
    go_ref[0] = g


def _rw_prep(p, mu_pad, w0, a0, k_k, k_a, w_up, a_up, g_up_pad):
    batch, t, _ = p.shape
    tm = min(512, t)
    nj = RW_WIDTH // LANE
    gw = RW_GATE_LORA_PAD

    def cur(col, width=LANE, tiled=True):
        base = col // width
        if tiled:
            return pl.BlockSpec((1, tm, width), lambda b, i, j: (b, i, base + j))
        return pl.BlockSpec((1, tm, width), lambda b, i, j: (b, i, base))

    def prev(col, width=LANE, tiled=True):
        base = col // width
        rb = tm // SUBLANE
        if tiled:
            return pl.BlockSpec((1, SUBLANE, width), lambda b, i, j: (b, jnp.maximum(i * rb - 1, 0), base + j))
        return pl.BlockSpec((1, SUBLANE, width), lambda b, i, j: (b, jnp.maximum(i * rb - 1, 0), base))

    def murow(col, width=LANE, tiled=True):
        base = col // width
        if tiled:
            return pl.BlockSpec((1, width), lambda b, i, j: (0, base + j))
        return pl.BlockSpec((1, width), lambda b, i, j: (0, base))

    vec = pl.BlockSpec((1, LANE), lambda b, i, j: (0, j))
    up = pl.BlockSpec((RW_LORA, LANE), lambda b, i, j: (0, j))
    gup = pl.BlockSpec((gw, LANE), lambda b, i, j: (0, j))
    out = pl.BlockSpec((1, tm, LANE), lambda b, i, j: (b, i, j))
    shp = jax.ShapeDtypeStruct((batch, t, RW_WIDTH), F32)
    return pl.pallas_call(
        functools.partial(_rw_prep_kernel, tm=tm),
        grid=(batch, t // tm, nj),
        in_specs=[
            cur(P_RW_R), cur(P_RW_K), cur(P_RW_V),
            cur(P_RW_WL, tiled=False), cur(P_RW_AL, tiled=False), cur(P_RW_GL, gw, tiled=False),
            prev(P_RW_R), prev(P_RW_K), prev(P_RW_V),
            prev(P_RW_WL, tiled=False), prev(P_RW_AL, tiled=False), prev(P_RW_GL, gw, tiled=False),
            murow(P_RW_R), murow(P_RW_K), murow(P_RW_V),
            murow(P_RW_WL, tiled=False), murow(P_RW_AL, tiled=False), murow(P_RW_GL, gw, tiled=False),
            vec, vec, vec, vec, up, up, gup,
        ],
        out_specs=[out] * 7,
        out_shape=[shp] * 7,
        compiler_params=_params("parallel", "parallel", "arbitrary"),
        name="rwkv7_prepare",
    )(p, p, p, p, p, p, p, p, p, p, p, p,
      mu_pad, mu_pad, mu_pad, mu_pad, mu_pad, mu_pad,
      w0, a0, k_k, k_a, w_up, a_up, g_up_pad)


RW_CHUNK = 64


def _rw_chunk(s_mat, r, k, v, kk, bb, lw, tri_incl, lower_incl, lower_strict, eye):
    lc = RW_CHUNK
    cum = _dot(tri_incl, lw, HIGHEST)
    e_inc = jnp.exp(cum)
    e_exc = jnp.exp(cum - lw)
    e_neg = jnp.exp(-cum)
    rt, at = r * e_inc, kk * e_exc
    kt, bt = k * e_neg, bb * e_neg
    lhs = jnp.concatenate([at, rt], axis=0)
    rhs = jnp.concatenate([bt, kt], axis=0)
    mm = _dot_nt(lhs, rhs, HIGHEST)
    m_ab = jnp.where(lower_strict, mm[:lc, :lc], 0.0)
    m_ak = jnp.where(lower_strict, mm[:lc, lc:], 0.0)
    m_rb = jnp.where(lower_incl, mm[lc:, :lc], 0.0)
    m_rk = jnp.where(lower_incl, mm[lc:, lc:], 0.0)

    pw = -m_ab
    t_inv = eye + pw
    steps = max(1, (lc - 1).bit_length() - 1)
    for _ in range(steps):
        pw = _dot(pw, pw, HIGHEST)
        t_inv = t_inv + _dot(t_inv, pw, HIGHEST)

    u = _dot(t_inv, _dot_nt(at, s_mat, HIGHEST) + _dot(m_ak, v, HIGHEST), HIGHEST)
    y = _dot_nt(rt, s_mat, HIGHEST) + _dot(m_rk, v, HIGHEST) - _dot(m_rb, u, HIGHEST)
    tail = jnp.exp(cum[lc - 1:lc, :] - cum)
    s_new = (s_mat * e_inc[lc - 1:lc, :] + _dot_tn(v, k * tail, HIGHEST) - _dot_tn(u, bb * tail, HIGHEST))
    return s_new, y


def _rw_kernel(r_ref, k_ref, v_ref, kk_ref, b_ref, lw_ref, g_ref, lnw_ref, lnb_ref, rk_ref, y_ref, *, t):
    lc = RW_CHUNK
    n = RW_HEAD_DIM
    row = lax.broadcasted_iota(jnp.int32, (lc, lc), 0)
    col = lax.broadcasted_iota(jnp.int32, (lc, lc), 1)
    lower_incl = col <= row
    lower_strict = col < row
    tri_incl = lower_incl.astype(F32)
    eye = (col == row).astype(F32)
    lnw, lnb, rkw = lnw_ref[...], lnb_ref[...], rk_ref[...]

    def body(c, carry):
        start = pl.multiple_of(c * lc, lc)
        sl = pl.ds(start, lc)
        r2, k2, v2 = r_ref[0, sl, :], k_ref[0, sl, :], v_ref[0, sl, :]
        kk2, b2, lw2 = kk_ref[0, sl, :], b_ref[0, sl, :], lw_ref[0, sl, :]
        new_states, ys = [], []
        for hh in range(LANE // n):
            lo, hi = hh * n, (hh + 1) * n
            r, k, v = r2[:, lo:hi], k2[:, lo:hi], v2[:, lo:hi]
            s_new, y = _rw_chunk(carry[hh], r, k, v, kk2[:, lo:hi], b2[:, lo:hi], lw2[:, lo:hi],
                                 tri_incl, lower_incl, lower_strict, eye)
            mean = jnp.mean(y, axis=-1, keepdims=True)
            var = jnp.mean(jnp.square(y - mean), axis=-1, keepdims=True)
            y = (y - mean) * lax.rsqrt(var + RW_LN_EPS)
            y = y * lnw[:, lo:hi] + lnb[:, lo:hi]
            y = y + jnp.sum(r * k * rkw[:, lo:hi], axis=-1, keepdims=True) * v
            new_states.append(s_new)
            ys.append(y)
        y_ref[0, sl, :] = (jnp.concatenate(ys, axis=-1) * g_ref[0, sl, :]).astype(y_ref.dtype)
        return tuple(new_states)

    init = tuple(jnp.zeros((n, n), F32) for _ in range(LANE // n))
    lax.fori_loop(0, t // lc, body, init)


def _rwkv7(r, k, v, kk, bb, lw, g, ln_w, ln_b, r_k):
    batch, t, _ = r.shape
    seq = pl.BlockSpec((1, t, LANE), lambda b, j: (b, 0, j))
    vec = pl.BlockSpec((1, LANE), lambda b, j: (0, j))
    return pl.pallas_call(
        functools.partial(_rw_kernel, t=t),
        grid=(batch, RW_WIDTH // LANE),
        in_specs=[seq] * 7 + [vec] * 3,
        out_specs=seq,
        out_shape=jax.ShapeDtypeStruct((batch, t, RW_WIDTH), BF16),
        compiler_params=_params("parallel", "parallel"),
        name="rwkv7_chunkwise",
    )(r, k, v, kk, bb, lw, g, ln_w, ln_b, r_k)


def _row(v, width=None):
    v = v.reshape(1, -1).astype(F32)
    if width is not None and v.shape[1] < width:
        v = jnp.pad(v, ((0, 0), (0, width - v.shape[1])))
    return v


def kernel(x, c, ada_w, ada_b, norm1, w_in, fox_f_bias, fox_norm, ml_conv_w, ml_conv_b, ml_i_bias, ml_f_bias,
           ml_norm, rw_mu, rw_w0, rw_w_up, rw_a0, rw_a_up, rw_g_up, rw_k_k, rw_k_a, rw_r_k, rw_ln_w, rw_ln_b,
           w_out, norm2, ffn_gate, ffn_up, ffn_down, final_norm):
    batch, t, d = x.shape
    depth = ada_w.shape[0]
    assert d == D_MODEL and t % ML_CHUNK == 0 and t % RW_CHUNK == 0

    mod = _modulation(c, ada_w, ada_b)
    mod = mod.reshape(depth, batch, 6, 1, d)

    for l in range(depth):
        sh1, sc1, g1, sh2, sc2, g2 = (mod[l, :, i] for i in range(6))

        h = _norm_mod(x, norm1[l], sc1, sh1, BF16)
        w_in_pad = _pad_columns(w_in[l]).astype(BF16)
        p = _matmul_plain(h, w_in_pad, F32, tm=1024, tn=512)

        fox_bias_row = _row(fox_f_bias[l], LANE)
        ml_bias_row = _row(jnp.concatenate([ml_i_bias[l], ml_f_bias[l]]), LANE)
        cum, li_all, bcs_all = _gates(p, fox_bias_row, ml_bias_row)
        cum_h = cum[:, :, :FOX_HEADS].transpose(0, 2, 1)
        y_fox = _fox_attention(p, cum_h[..., None], cum_h[:, :, None, :], _row(fox_norm[l]))

        li = li_all[:, :, :ML_HEADS]
        bcs = bcs_all[:, :, ML_HEADS:2 * ML_HEADS]
        gate_cols = jnp.stack([li, bcs], axis=-1).transpose(0, 2, 1, 3)
        gate_rows = gate_cols.transpose(0, 1, 3, 2)
        y_ml = _mlstm(p, gate_rows, gate_cols, ml_conv_w[l], _row(ml_conv_b[l]), _row(ml_norm[l]))

        mu_full = jnp.zeros((1, N_IN), F32).at[:, _O_RW:].set(rw_mu[l][None, :])
        mu_pad = _pad_columns(mu_full)
        g_up_pad = jnp.pad(rw_g_up[l], ((0, RW_GATE_LORA_PAD - RW_GATE_LORA), (0, 0)))
        rw = _rw_prep(p, mu_pad, _row(rw_w0[l]), _row(rw_a0[l]), _row(rw_k_k[l]), _row(rw_k_a[l]),
                      rw_w_up[l], rw_a_up[l], g_up_pad)
        y_rw = _rwkv7(*rw, _row(rw_ln_w[l]), _row(rw_ln_b[l]), _row(rw_r_k[l]))

        mix = jnp.concatenate([y_fox, y_ml, y_rw], axis=-1)
        x = _matmul_residual(mix, w_out[l].astype(BF16), x, g1, tm=1024, tn=512, tk=d, name="output_projection")

        h = _norm_mod(x, norm2[l], sc2, sh2, BF16)
        pad_ff = D_FF_PAD - D_FF
        wg = jnp.pad(ffn_gate[l].astype(BF16), ((0, 0), (0, pad_ff)))
        wu = jnp.pad(ffn_up[l].astype(BF16), ((0, 0), (0, pad_ff)))
        wd = jnp.pad(ffn_down[l].astype(BF16), ((0, pad_ff), (0, 0)))
        act = _matmul_swiglu(h, wg, wu, tm=1024, tn=512)
        x = _matmul_residual(act, wd, x, g2, tm=1024, tn=512, tk=D_FF_PAD // 4, name="ffn_down")

    zeros = jnp.zeros((batch, 1, d), F32)
    return _norm_mod(x, final_norm, zeros, zeros, F32)
```

```python
import functools

import jax
import jax.numpy as jnp
from jax import lax
from jax.experimental import pallas as pl
from jax.experimental.pallas import tpu as pltpu

F32 = jnp.float32
BF16 = jnp.bfloat16
HIGHEST = lax.Precision.HIGHEST

D_MODEL = 4096
FOX_HEADS, FOX_HEAD_DIM = 12, 128
FOX_WIDTH = FOX_HEADS * FOX_HEAD_DIM
ML_HEADS, ML_QK_DIM, ML_V_DIM = 4, 128, 256
ML_QK_WIDTH = ML_HEADS * ML_QK_DIM
ML_WIDTH = ML_HEADS * ML_V_DIM
ML_CONV = 4
GATE_SOFTCAP = 15.0
RW_HEADS, RW_HEAD_DIM = 24, 64
RW_WIDTH = RW_HEADS * RW_HEAD_DIM
RW_LORA = 128
RW_GATE_LORA = 480
RW_GATE_LORA_PAD = 512
RW_LN_EPS = 64e-5
D_FF = 11008
D_FF_PAD = 11264
NORM_EPS = 1e-6

LANE = 128
SUBLANE = 8
VMEM_LIMIT = 56 * 1024 * 1024

_O_FOX_Q, _O_FOX_K, _O_FOX_V, _O_FOX_F = 0, 1536, 3072, 4608
_O_ML = 4620
_O_ML_QK, _O_ML_V, _O_ML_I, _O_ML_F, _O_ML_O = _O_ML, _O_ML + 1024, _O_ML + 2048, _O_ML + 2052, _O_ML + 2056
_O_RW = 7700
_O_RW_R, _O_RW_K, _O_RW_V = _O_RW, _O_RW + 1536, _O_RW + 3072
_O_RW_WL, _O_RW_AL, _O_RW_GL = _O_RW + 4608, _O_RW + 4736, _O_RW + 4864
N_IN = 13044
P_FOX_Q, P_FOX_K, P_FOX_V = 0, 1536, 3072
P_ML_V, P_ML_O, P_ML_Q, P_ML_K = 4608, 5632, 6656, 7168
P_RW_R, P_RW_K, P_RW_V = 7680, 9216, 10752
P_RW_GL, P_RW_WL, P_RW_AL = 12288, 12800, 12928
P_FOX_F, P_ML_IF = 13056, 13184
N_PAD = 13312
_SECTIONS = (
    (P_FOX_Q, _O_FOX_Q, 3 * FOX_WIDTH),
    (P_ML_V, _O_ML_V, ML_WIDTH),
    (P_ML_O, _O_ML_O, ML_WIDTH),
    (P_ML_Q, _O_ML_QK, 2 * ML_QK_WIDTH),
    (P_RW_R, _O_RW_R, 3 * RW_WIDTH),
    (P_RW_GL, _O_RW_GL, RW_GATE_LORA),
    (P_RW_WL, _O_RW_WL, 2 * RW_LORA),
    (P_FOX_F, _O_FOX_F, FOX_HEADS),
    (P_ML_IF, _O_ML_I, 2 * ML_HEADS),
)


def _pad_columns(w):
    parts, pos = [], 0
    for p_start, o_start, width in _SECTIONS:
        if p_start > pos:
            parts.append(jnp.zeros(w.shape[:-1] + (p_start - pos,), w.dtype))
        parts.append(w[..., o_start:o_start + width])
        pos = p_start + width
    if pos < N_PAD:
        parts.append(jnp.zeros(w.shape[:-1] + (N_PAD - pos,), w.dtype))
    return jnp.concatenate(parts, axis=-1)


def _params(*sem):
    return pltpu.CompilerParams(dimension_semantics=sem, vmem_limit_bytes=VMEM_LIMIT)


def _log_sigmoid(x):
    return jnp.minimum(x, 0.0) - jnp.log1p(jnp.exp(-jnp.abs(x)))


def _sigmoid(x):
    return 1.0 / (1.0 + jnp.exp(-x))


def _silu(x):
    return x * _sigmoid(x)


def _dot(a, b, precision=None):
    return jnp.dot(a, b, preferred_element_type=F32, precision=precision)


def _dot_nt(a, b, precision=None):
    return lax.dot_general(a, b, (((1,), (1,)), ((), ())), preferred_element_type=F32, precision=precision)


def _dot_tn(a, b, precision=None):
    return lax.dot_general(a, b, (((0,), (0,)), ((), ())), preferred_element_type=F32, precision=precision)


def _mod_kernel(c_ref, w_ref, b_ref, o_ref):
    a = _silu(c_ref[...]).astype(BF16)
    o_ref[0] = _dot(a, w_ref[0].astype(BF16)) + b_ref[0]


def _modulation(c, ada_w, ada_b):
    depth, d, n = ada_w.shape
    batch = c.shape[0]
    rows = -(-batch // SUBLANE) * SUBLANE
    c_pad = jnp.zeros((rows, d), F32).at[:batch].set(c)
    tn = 512
    out = pl.pallas_call(
        _mod_kernel,
        grid=(depth, n // tn),
        in_specs=[
            pl.BlockSpec((rows, d), lambda l, j: (0, 0)),
            pl.BlockSpec((1, d, tn), lambda l, j: (l, 0, j)),
            pl.BlockSpec((1, 1, tn), lambda l, j: (l, 0, j)),
        ],
        out_specs=pl.BlockSpec((1, rows, tn), lambda l, j: (l, 0, j)),
        out_shape=jax.ShapeDtypeStruct((depth, rows, n), F32),
        compiler_params=_params("parallel", "parallel"),
        name="adaln_modulation",
    )(c_pad, ada_w, ada_b.reshape(depth, 1, n))
    return out[:, :batch]


def _norm_kernel(x_ref, g_ref, sc_ref, sh_ref, o_ref):
    x = x_ref[0]
    y = x * lax.rsqrt(jnp.mean(x * x, axis=-1, keepdims=True) + NORM_EPS)
    y = (y * g_ref[...]) * (1.0 + sc_ref[0]) + sh_ref[0]
    o_ref[0] = y.astype(o_ref.dtype)


def _norm_mod(x, g, scale, shift, out_dtype):
    batch, t, d = x.shape
    tm = min(512, t)
    return pl.pallas_call(
        _norm_kernel,
        grid=(batch, t // tm),
        in_specs=[
            pl.BlockSpec((1, tm, d), lambda b, i: (b, i, 0)),
            pl.BlockSpec((1, d), lambda b, i: (0, 0)),
            pl.BlockSpec((1, 1, d), lambda b, i: (b, 0, 0)),
            pl.BlockSpec((1, 1, d), lambda b, i: (b, 0, 0)),
        ],
        out_specs=pl.BlockSpec((1, tm, d), lambda b, i: (b, i, 0)),
        out_shape=jax.ShapeDtypeStruct((batch, t, d), out_dtype),
        compiler_params=_params("parallel", "parallel"),
        name="rmsnorm_modulate",
    )(x, g.reshape(1, d), scale, shift)


def _mm_plain_kernel(a_ref, w_ref, o_ref):
    o_ref[0] = _dot(a_ref[0], w_ref[...]).astype(o_ref.dtype)


def _mm_swiglu_kernel(a_ref, wg_ref, wu_ref, o_ref):
    a = a_ref[0]
    gate = _dot(a, wg_ref[...])
    up = _dot(a, wu_ref[...])
    o_ref[0] = (_silu(gate) * up).astype(o_ref.dtype)


def _mm_residual_kernel(a_ref, w_ref, x_ref, g_ref, o_ref, acc_ref, *, nk):
    k = pl.program_id(3)

    @pl.when(k == 0)
    def _():
        acc_ref[...] = jnp.zeros_like(acc_ref)

    acc_ref[...] += _dot(a_ref[0], w_ref[...])

    @pl.when(k == nk - 1)
    def _():
        o_ref[0] = x_ref[0] + g_ref[0] * acc_ref[...]


def _matmul_plain(a, w, out_dtype, tm, tn):
    batch, t, kdim = a.shape
    n = w.shape[1]
    tm = min(tm, t)
    return pl.pallas_call(
        _mm_plain_kernel,
        grid=(batch, t // tm, n // tn),
        in_specs=[
            pl.BlockSpec((1, tm, kdim), lambda b, i, j: (b, i, 0)),
            pl.BlockSpec((kdim, tn), lambda b, i, j: (0, j)),
        ],
        out_specs=pl.BlockSpec((1, tm, tn), lambda b, i, j: (b, i, j)),
        out_shape=jax.ShapeDtypeStruct((batch, t, n), out_dtype),
        compiler_params=_params("parallel", "parallel", "arbitrary"),
        name="input_projection",
    )(a, w)


def _matmul_swiglu(a, wg, wu, tm, tn):
    batch, t, kdim = a.shape
    n = wg.shape[1]
    tm = min(tm, t)
    return pl.pallas_call(
        _mm_swiglu_kernel,
        grid=(batch, t // tm, n // tn),
        in_specs=[
            pl.BlockSpec((1, tm, kdim), lambda b, i, j: (b, i, 0)),
            pl.BlockSpec((kdim, tn), lambda b, i, j: (0, j)),
            pl.BlockSpec((kdim, tn), lambda b, i, j: (0, j)),
        ],
        out_specs=pl.BlockSpec((1, tm, tn), lambda b, i, j: (b, i, j)),
        out_shape=jax.ShapeDtypeStruct((batch, t, n), BF16),
        compiler_params=_params("parallel", "parallel", "arbitrary"),
        name="ffn_gate_up_swiglu",
    )(a, wg, wu)


def _matmul_residual(a, w, x, gate, tm, tn, tk, name):
    batch, t, kdim = a.shape
    n = w.shape[1]
    tm = min(tm, t)
    nk = kdim // tk
    return pl.pallas_call(
        functools.partial(_mm_residual_kernel, nk=nk),
        grid=(batch, t // tm, n // tn, nk),
        in_specs=[
            pl.BlockSpec((1, tm, tk), lambda b, i, j, k: (b, i, k)),
            pl.BlockSpec((tk, tn), lambda b, i, j, k: (k, j)),
            pl.BlockSpec((1, tm, tn), lambda b, i, j, k: (b, i, j)),
            pl.BlockSpec((1, 1, tn), lambda b, i, j, k: (b, 0, j)),
        ],
        out_specs=pl.BlockSpec((1, tm, tn), lambda b, i, j, k: (b, i, j)),
        out_shape=jax.ShapeDtypeStruct((batch, t, n), F32),
        scratch_shapes=[pltpu.VMEM((tm, tn), F32)],
        compiler_params=_params("parallel", "parallel", "parallel", "arbitrary"),
        name=name,
    )(a, w, x, gate)


ML_CHUNK = 128
_CUM_ROWS = 256


def _gates_kernel(ff_ref, mif_ref, fb_ref, mb_ref, cum_ref, li_ref, bcs_ref, *, t):
    r = _CUM_ROWS if t % _CUM_ROWS == 0 else t
    row = lax.broadcasted_iota(jnp.int32, (r, r), 0)
    col = lax.broadcasted_iota(jnp.int32, (r, r), 1)
    tril = (col <= row).astype(F32)
    fb = fb_ref[...]

    def cum_body(i, carry):
        start = pl.multiple_of(i * r, r)
        lg = _log_sigmoid(ff_ref[0, pl.ds(start, r), :] + fb)
        cs = _dot(tril, lg, HIGHEST) + carry
        cum_ref[0, pl.ds(start, r), :] = cs
        return cs[r - 1:r, :]

    lax.fori_loop(0, t // r, cum_body, jnp.zeros((1, LANE), F32))

    lc = ML_CHUNK
    rowc = lax.broadcasted_iota(jnp.int32, (lc, lc), 0)
    colc = lax.broadcasted_iota(jnp.int32, (lc, lc), 1)
    trilc = (colc <= rowc).astype(F32)
    mb = mb_ref[...]

    def ml_body(i, carry):
        start = pl.multiple_of(i * lc, lc)
        z = mif_ref[0, pl.ds(start, lc), :] + mb
        capped = GATE_SOFTCAP * jnp.tanh(z / GATE_SOFTCAP)
        li_ref[0, pl.ds(start, lc), :] = capped
        bcs_ref[0, pl.ds(start, lc), :] = _dot(trilc, _log_sigmoid(capped), HIGHEST)
        return carry

    lax.fori_loop(0, t // lc, ml_body, 0)


def _gates(p, fox_bias_row, ml_bias_row):
    batch, t, _ = p.shape
    blk = lambda col: pl.BlockSpec((1, t, LANE), lambda b: (b, 0, col // LANE))
    row = pl.BlockSpec((1, LANE), lambda b: (0, 0))
    out = pl.BlockSpec((1, t, LANE), lambda b: (b, 0, 0))
    shp = jax.ShapeDtypeStruct((batch, t, LANE), F32)
    return pl.pallas_call(
        functools.partial(_gates_kernel, t=t),
        grid=(batch,),
        in_specs=[blk(P_FOX_F), blk(P_ML_IF), row, row],
        out_specs=[out, out, out],
        out_shape=[shp, shp, shp],
        compiler_params=_params("parallel"),
        name="gate_cumsums",
    )(p, p, fox_bias_row, ml_bias_row)


def _fox_kernel(q_ref, k_ref, v_ref, fq_ref, fk_ref, g_ref, o_ref, *, tq):
    i = pl.program_id(2)
    scale = FOX_HEAD_DIM ** -0.5
    q = (q_ref[0] * scale).astype(BF16)
    fq = fq_ref[0, 0]
    q_pos = i * tq + lax.broadcasted_iota(jnp.int32, (tq, 1), 0)

    def body(j, carry):
        m, l, acc = carry
        start = pl.multiple_of(j * tq, tq)
        ks = k_ref[0, pl.ds(start, tq), :].astype(BF16)
        vs = v_ref[0, pl.ds(start, tq), :].astype(BF16)
        fk = fk_ref[0, 0, :, pl.ds(start, tq)]
        s = _dot_nt(q, ks) + (fq - fk)
        k_pos = start + lax.broadcasted_iota(jnp.int32, (1, tq), 1)
        s = jnp.where(k_pos <= q_pos, s, -jnp.inf)
        m_new = jnp.maximum(m, jnp.max(s, axis=-1, keepdims=True))
        p = jnp.exp(s - m_new)
        alpha = jnp.exp(m - m_new)
        l = alpha * l + jnp.sum(p, axis=-1, keepdims=True)
        acc = alpha * acc + _dot(p.astype(BF16), vs)
        return m_new, l, acc

    init = (jnp.full((tq, 1), -1e30, F32), jnp.zeros((tq, 1), F32), jnp.zeros((tq, FOX_HEAD_DIM), F32))
    _, l, acc = lax.fori_loop(0, i + 1, body, init)
    o = acc / l
    o = o * lax.rsqrt(jnp.mean(o * o, axis=-1, keepdims=True) + NORM_EPS)
    o_ref[0] = (o * g_ref[...]).astype(o_ref.dtype)


def _fox_attention(p, cum_col, cum_row, out_g):
    batch, t, _ = p.shape
    tq = min(512, t)
    dh = FOX_HEAD_DIM
    return pl.pallas_call(
        functools.partial(_fox_kernel, tq=tq),
        grid=(batch, FOX_HEADS, t // tq),
        in_specs=[
            pl.BlockSpec((1, tq, dh), lambda b, h, i: (b, i, P_FOX_Q // dh + h)),
            pl.BlockSpec((1, t, dh), lambda b, h, i: (b, 0, P_FOX_K // dh + h)),
            pl.BlockSpec((1, t, dh), lambda b, h, i: (b, 0, P_FOX_V // dh + h)),
            pl.BlockSpec((1, 1, tq, 1), lambda b, h, i: (b, h, i, 0)),
            pl.BlockSpec((1, 1, 1, t), lambda b, h, i: (b, h, 0, 0)),
            pl.BlockSpec((1, dh), lambda b, h, i: (0, h)),
        ],
        out_specs=pl.BlockSpec((1, tq, dh), lambda b, h, i: (b, i, h)),
        out_shape=jax.ShapeDtypeStruct((batch, t, FOX_WIDTH), BF16),
        compiler_params=_params("parallel", "parallel", "arbitrary"),
        name="fox_attention",
    )(p, p, p, cum_col, cum_row, out_g)


def _mlstm_kernel(q_ref, k_ref, v_ref, o_ref, grow_ref, gcol_ref, cwq_ref, cwk_ref, cbq_ref, cbk_ref,
                  ng_ref, y_ref, *, t):
    lc = ML_CHUNK
    dk, dv = ML_QK_DIM, ML_V_DIM
    row = lax.broadcasted_iota(jnp.int32, (lc, lc), 0)
    col = lax.broadcasted_iota(jnp.int32, (lc, lc), 1)
    causal = col <= row
    cwq, cwk = cwq_ref[...], cwk_ref[...]
    cbq, cbk = cbq_ref[...], cbk_ref[...]
    norm_g = ng_ref[...]

    def conv_silu(ref, start, prev_ok, w, b):
        cur = ref[0, pl.ds(start, lc), :]
        prev = ref[0, pl.ds(jnp.maximum(start - SUBLANE, 0), SUBLANE), :]
        prev = jnp.where(prev_ok, prev, 0.0)
        ext = jnp.concatenate([prev, cur], axis=0)
        out = b
        for j in range(ML_CONV):
            off = SUBLANE - (ML_CONV - 1) + j
            out = out + ext[off:off + lc, :] * w[j:j + 1, :]
        return _silu(out)

    def body(c, carry):
        cmat, nvec, m = carry
        start = pl.multiple_of(c * lc, lc)
        prev_ok = c > 0
        qt = conv_silu(q_ref, start, prev_ok, cwq, cbq) * (dk ** -0.5)
        kt = conv_silu(k_ref, start, prev_ok, cwk, cbk)
        vt = v_ref[0, pl.ds(start, lc), :]
        li_row = grow_ref[0, 0, 0:1, pl.ds(start, lc)]
        b_row = grow_ref[0, 0, 1:2, pl.ds(start, lc)]
        li_col = gcol_ref[0, 0, pl.ds(start, lc), 0:1]
        b_col = gcol_ref[0, 0, pl.ds(start, lc), 1:2]
        g = b_row[:, lc - 1:lc]

        a_inter = b_col + m
        dm = jnp.where(causal, b_col - b_row + li_row, -jnp.inf)
        m_t = jnp.maximum(a_inter, jnp.max(dm, axis=-1, keepdims=True))
        w_inter = jnp.exp(a_inter - m_t)
        qb, kb = qt.astype(BF16), kt.astype(BF16)
        s = _dot_nt(qb, kb) * jnp.exp(dm - m_t)
        num = w_inter * _dot(qb, cmat.astype(BF16)) + _dot(s.astype(BF16), vt.astype(BF16))
        den = w_inter * jnp.sum(qt * nvec, axis=-1, keepdims=True) + jnp.sum(s, axis=-1, keepdims=True)
        h = num / jnp.maximum(jnp.abs(den), jnp.exp(-m_t))

        upd = g - b_col + li_col
        m_new = jnp.maximum(g + m, jnp.max(upd, axis=0, keepdims=True))
        decay = jnp.exp(g + m - m_new)
        wk = jnp.exp(upd - m_new) * kt
        cmat = decay * cmat + _dot_tn(wk.astype(BF16), vt.astype(BF16))
        nvec = decay * nvec + jnp.sum(wk, axis=0, keepdims=True)

        hn = h * lax.rsqrt(jnp.mean(h * h, axis=-1, keepdims=True) + NORM_EPS) * norm_g
        og = o_ref[0, pl.ds(start, lc), :]
        y_ref[0, pl.ds(start, lc), :] = (hn * _sigmoid(og)).astype(y_ref.dtype)
        return cmat, nvec, m_new

    init = (jnp.zeros((dk, dv), F32), jnp.zeros((1, dk), F32), jnp.zeros((1, 1), F32))
    lax.fori_loop(0, t // lc, body, init)


def _mlstm(p, gate_rows, gate_cols, conv_w, conv_b, norm_g):
    batch, t, _ = p.shape
    dk, dv = ML_QK_DIM, ML_V_DIM
    nh = ML_HEADS
    return pl.pallas_call(
        functools.partial(_mlstm_kernel, t=t),
        grid=(batch, nh),
        in_specs=[
            pl.BlockSpec((1, t, dk), lambda b, h: (b, 0, P_ML_Q // dk + h)),
            pl.BlockSpec((1, t, dk), lambda b, h: (b, 0, P_ML_K // dk + h)),
            pl.BlockSpec((1, t, dv), lambda b, h: (b, 0, P_ML_V // dv + h)),
            pl.BlockSpec((1, t, dv), lambda b, h: (b, 0, P_ML_O // dv + h)),
            pl.BlockSpec((1, 1, 2, t), lambda b, h: (b, h, 0, 0)),
            pl.BlockSpec((1, 1, t, 2), lambda b, h: (b, h, 0, 0)),
            pl.BlockSpec((ML_CONV, dk), lambda b, h: (0, h)),
            pl.BlockSpec((ML_CONV, dk), lambda b, h: (0, nh + h)),
            pl.BlockSpec((1, dk), lambda b, h: (0, h)),
            pl.BlockSpec((1, dk), lambda b, h: (0, nh + h)),
            pl.BlockSpec((1, dv), lambda b, h: (0, h)),
        ],
        out_specs=pl.BlockSpec((1, t, dv), lambda b, h: (b, 0, h)),
        out_shape=jax.ShapeDtypeStruct((batch, t, ML_WIDTH), BF16),
        compiler_params=_params("parallel", "parallel"),
        name="mlstm_chunkwise",
    )(p, p, p, p, gate_rows, gate_cols, conv_w, conv_w, conv_b, conv_b, norm_g)


RW_CHUNK = 64
RW_TILE = 512


def _rw_kernel(r_ref, k_ref, v_ref, wl_ref, al_ref, gl_ref,
               rp_ref, kp_ref, vp_ref, wlp_ref, alp_ref, glp_ref,
               mur_ref, muk_ref, muv_ref, muw_ref, mua_ref, mug_ref,
               w0_ref, a0_ref, kkw_ref, kaw_ref, rkw_ref, lnw_ref, lnb_ref,
               wup_ref, aup_ref, gup_ref, y_ref, s_ref, *, tm):
    i = pl.program_id(2)
    lc = RW_CHUNK
    n = RW_HEAD_DIM
    pair = LANE

    @pl.when(i == 0)
    def _():
        s_ref[...] = jnp.zeros_like(s_ref)

    first = lax.broadcasted_iota(jnp.int32, (tm, 1), 0) == 0
    has_prev = i > 0

    def shift(cur_ref, prev_ref, mu_ref):
        cur = cur_ref[0]
        last = jnp.where(has_prev, prev_ref[0][SUBLANE - 1:SUBLANE, :], 0.0)
        prev = jnp.where(first, last, pltpu.roll(cur, 1, axis=0))
        return cur + (prev - cur) * mu_ref[...]

    r = shift(r_ref, rp_ref, mur_ref)
    k = shift(k_ref, kp_ref, muk_ref)
    v = shift(v_ref, vp_ref, muv_ref)
    wl = shift(wl_ref, wlp_ref, muw_ref)
    al = shift(al_ref, alp_ref, mua_ref)
    gl = shift(gl_ref, glp_ref, mug_ref)

    hrow = lax.broadcasted_iota(jnp.int32, (pair, pair), 0)
    hcol = lax.broadcasted_iota(jnp.int32, (pair, pair), 1)
    same_head = jnp.where(hrow // n == hcol // n, 1.0, 0.0).astype(BF16)

    def head_sum(x):
        hi = x.astype(BF16)
        lo = (x - hi.astype(F32)).astype(BF16)
        return _dot(hi, same_head) + _dot(lo, same_head)

    z = w0_ref[...] + _dot(jnp.tanh(wl).astype(BF16), wup_ref[...])
    lw = -jnp.exp(_log_sigmoid(z) - 0.5)
    a = _sigmoid(a0_ref[...] + _dot(al.astype(BF16), aup_ref[...]))
    gate = _dot(_sigmoid(gl).astype(BF16), gup_ref[...])
    kk = k * kkw_ref[...]
    kk = kk / jnp.maximum(jnp.sqrt(head_sum(kk * kk)), 1e-12)
    k = k * (1.0 + (a - 1.0) * kaw_ref[...])
    bb = kk * a
    bonus = head_sum(r * k * rkw_ref[...]) * v

    trow = lax.broadcasted_iota(jnp.int32, (lc, lc), 0)
    tcol = lax.broadcasted_iota(jnp.int32, (lc, lc), 1)
    tri = jnp.where(tcol <= trow, 1.0, 0.0).astype(BF16)
    lw_hi = lw.astype(BF16)
    rem = lw - lw_hi.astype(F32)
    lw_mid = rem.astype(BF16)
    lw_lo = (rem - lw_mid.astype(F32)).astype(BF16)
    nchunk = tm // lc
    cums, ends = [], []
    for c in range(nchunk):
        sl = slice(c * lc, (c + 1) * lc)
        cum_c = _dot(tri, lw_hi[sl]) + (_dot(tri, lw_mid[sl]) + _dot(tri, lw_lo[sl]))
        cums.append(cum_c)
        ends.append(jnp.broadcast_to(cum_c[lc - 1:lc, :], (lc, LANE)))
    cum = jnp.concatenate(cums, axis=0)
    cum_end = jnp.concatenate(ends, axis=0)
    e_inc = jnp.exp(cum)
    e_neg = jnp.exp(-cum)
    tail = jnp.exp(cum_end - cum)
    rt = r * e_inc
    at = kk * jnp.exp(cum - lw)
    kt = k * e_neg
    bt = bb * e_neg
    kh = k * tail
    bh = bb * tail

    head0 = lax.broadcasted_iota(jnp.int32, (1, LANE), 1) < n

    def stack(x, dtype=BF16):
        return jnp.concatenate([jnp.where(head0, x, 0.0), jnp.where(head0, 0.0, x)], axis=0).astype(dtype)

    prow = lax.broadcasted_iota(jnp.int32, (pair, pair), 0)
    pcol = lax.broadcasted_iota(jnp.int32, (pair, pair), 1)
    lower_strict = pcol < prow
    lower_incl = pcol <= prow
    eye = jnp.where(pcol == prow, 1.0, 0.0)

    chunks = range(nchunk)
    rows = [slice(c * lc, (c + 1) * lc) for c in chunks]
    a2 = [stack(at[sl]) for sl in rows]
    r2 = [stack(rt[sl], F32) for sl in rows]
    v2 = [stack(v[sl]) for sl in rows]
    mm = [_dot_nt(jnp.concatenate([a2[c], r2[c].astype(BF16)], axis=0),
                  jnp.concatenate([stack(bt[rows[c]]), stack(kt[rows[c]])], axis=0)) for c in chunks]
    m_ak = [jnp.where(lower_strict, mm[c][:pair, pair:], 0.0).astype(BF16) for c in chunks]
    m_rb = [jnp.where(lower_incl, mm[c][pair:, :pair], 0.0).astype(BF16) for c in chunks]
    m_rk = [jnp.where(lower_incl, mm[c][pair:, pair:], 0.0).astype(BF16) for c in chunks]

    m_ab = [jnp.where(lower_strict, mm[c][:pair, :pair], 0.0) for c in chunks]
    i_plus_m = [(eye + m_ab[c]).astype(BF16) for c in chunks]
    t_inv = [eye - m_ab[c] for c in chunks]
    for _ in range((lc - 1).bit_length() - 1):
        res = [_dot(i_plus_m[c], t_inv[c].astype(BF16)) for c in chunks]
        t_inv = [_dot(t_inv[c].astype(BF16), (2.0 * eye - res[c]).astype(BF16)) for c in chunks]

    x1 = [_dot(m_ak[c], v2[c]) for c in chunks]
    aub = [_dot(t_inv[c].astype(BF16), jnp.concatenate([a2[c], x1[c].astype(BF16)], axis=1)).astype(BF16)
           for c in chunks]
    rb = [_dot(m_rb[c], aub[c]) for c in chunks]
    r_eff = [r2[c] - rb[c][:, :pair] for c in chunks]
    y_loc = [_dot(m_rk[c], v2[c]) - rb[c][:, pair:] for c in chunks]
    r_eff = [(r_eff[c][:lc] + r_eff[c][lc:]).astype(BF16) for c in chunks]
    y_loc = [y_loc[c][:lc] + y_loc[c][lc:] for c in chunks]
    gd = [_dot_tn(aub[c], stack(bh[rows[c]])) for c in chunks]
    g_mat = [(eye * e_inc[(c + 1) * lc - 1:(c + 1) * lc, :] - gd[c][:pair]).astype(BF16) for c in chunks]
    d_mat = [_dot_tn(v2[c], stack(kh[rows[c]])) - gd[c][pair:] for c in chunks]

    s_mat = s_ref[...]
    ys = []
    for c in chunks:
        s_b = s_mat.astype(BF16)
        ys.append(_dot_nt(r_eff[c], s_b) + y_loc[c])
        s_mat = _dot(s_b, g_mat[c]) + d_mat[c]
    s_ref[...] = s_mat

    y = jnp.concatenate(ys, axis=0)
    mean = head_sum(y) * (1.0 / n)
    cen = y - mean
    var = head_sum(cen * cen) * (1.0 / n)
    y = cen * lax.rsqrt(var + RW_LN_EPS) * lnw_ref[...] + lnb_ref[...]
    y_ref[0] = ((y + bonus) * gate).astype(y_ref.dtype)


def _rwkv7(p, mu_pad, w0, a0, k_k, k_a, r_k, ln_w, ln_b, w_up, a_up, g_up_pad):
    batch, t, _ = p.shape
    tm = min(RW_TILE, t)
    nj = RW_WIDTH // LANE
    gw = RW_GATE_LORA_PAD
    rb = tm // SUBLANE

    def cur(col, width=LANE, tiled=True):
        base = col // width
        if tiled:
            return pl.BlockSpec((1, tm, width), lambda b, j, i: (b, i, base + j))
        return pl.BlockSpec((1, tm, width), lambda b, j, i: (b, i, base))

    def prev(col, width=LANE, tiled=True):
        base = col // width
        if tiled:
            return pl.BlockSpec((1, SUBLANE, width), lambda b, j, i: (b, jnp.maximum(i * rb - 1, 0), base + j))
        return pl.BlockSpec((1, SUBLANE, width), lambda b, j, i: (b, jnp.maximum(i * rb - 1, 0), base))

    def murow(col, width=LANE, tiled=True):
        base = col // width
        if tiled:
            return pl.BlockSpec((1, width), lambda b, j, i: (0, base + j))
        return pl.BlockSpec((1, width), lambda b, j, i: (0, base))

    vec = pl.BlockSpec((1, LANE), lambda b, j, i: (0, j))
    up = pl.BlockSpec((RW_LORA, LANE), lambda b, j, i: (0, j))
    gup = pl.BlockSpec((gw, LANE), lambda b, j, i: (0, j))
    return pl.pallas_call(
        functools.partial(_rw_kernel, tm=tm),
        grid=(batch, nj, t // tm),
        in_specs=[
            cur(P_RW_R), cur(P_RW_K), cur(P_RW_V),
            cur(P_RW_WL, tiled=False), cur(P_RW_AL, tiled=False), cur(P_RW_GL, gw, tiled=False),
            prev(P_RW_R), prev(P_RW_K), prev(P_RW_V),
            prev(P_RW_WL, tiled=False), prev(P_RW_AL, tiled=False), prev(P_RW_GL, gw, tiled=False),
            murow(P_RW_R), murow(P_RW_K), murow(P_RW_V),
            murow(P_RW_WL, tiled=False), murow(P_RW_AL, tiled=False), murow(P_RW_GL, gw, tiled=False),
            vec, vec, vec, vec, vec, vec, vec, up, up, gup,
        ],
        out_specs=pl.BlockSpec((1, tm, LANE), lambda b, j, i: (b, i, j)),
        out_shape=jax.ShapeDtypeStruct((batch, t, RW_WIDTH), BF16),
        scratch_shapes=[pltpu.VMEM((LANE, LANE), F32)],
        compiler_params=_params("parallel", "parallel", "arbitrary"),
        name="rwkv7_chunkwise",
    )(p, p, p, p, p, p, p, p, p, p, p, p,
      mu_pad, mu_pad, mu_pad, mu_pad, mu_pad, mu_pad,
      w0, a0, k_k, k_a, r_k, ln_w, ln_b, w_up, a_up, g_up_pad)


def _row(v, width=None):
    v = v.reshape(1, -1).astype(F32)
    if width is not None and v.shape[1] < width:
        v = jnp.pad(v, ((0, 0), (0, width - v.shape[1])))
    return v


def kernel(x, c, ada_w, ada_b, norm1, w_in, fox_f_bias, fox_norm, ml_conv_w, ml_conv_b, ml_i_bias, ml_f_bias,
           ml_norm, rw_mu, rw_w0, rw_w_up, rw_a0, rw_a_up, rw_g_up, rw_k_k, rw_k_a, rw_r_k, rw_ln_w, rw_ln_b,
           w_out, norm2, ffn_gate, ffn_up, ffn_down, final_norm):
    batch, t, d = x.shape
    depth = ada_w.shape[0]
    assert d == D_MODEL and t % ML_CHUNK == 0 and t % RW_CHUNK == 0

    mod = _modulation(c, ada_w, ada_b)
    mod = mod.reshape(depth, batch, 6, 1, d)

    for l in range(depth):
        sh1, sc1, g1, sh2, sc2, g2 = (mod[l, :, i] for i in range(6))

        h = _norm_mod(x, norm1[l], sc1, sh1, BF16)
        w_in_pad = lax.optimization_barrier(_pad_columns(w_in[l])).astype(BF16)
        p = _matmul_plain(h, w_in_pad, F32, tm=1024, tn=512)

        fox_bias_row = _row(fox_f_bias[l], LANE)
        ml_bias_row = _row(jnp.concatenate([ml_i_bias[l], ml_f_bias[l]]), LANE)
        cum, li_all, bcs_all = _gates(p, fox_bias_row, ml_bias_row)
        cum_h = cum[:, :, :FOX_HEADS].transpose(0, 2, 1)
        y_fox = _fox_attention(p, cum_h[..., None], cum_h[:, :, None, :], _row(fox_norm[l]))

        li = li_all[:, :, :ML_HEADS]
        bcs = bcs_all[:, :, ML_HEADS:2 * ML_HEADS]
        gate_cols = jnp.stack([li, bcs], axis=-1).transpose(0, 2, 1, 3)
        gate_rows = gate_cols.transpose(0, 1, 3, 2)
        y_ml = _mlstm(p, gate_rows, gate_cols, ml_conv_w[l], _row(ml_conv_b[l]), _row(ml_norm[l]))

        mu_full = jnp.zeros((1, N_IN), F32).at[:, _O_RW:].set(rw_mu[l][None, :])
        mu_pad = _pad_columns(mu_full)
        g_up_pad = jnp.pad(rw_g_up[l], ((0, RW_GATE_LORA_PAD - RW_GATE_LORA), (0, 0))).astype(BF16)
        y_rw = _rwkv7(p, mu_pad, _row(rw_w0[l]), _row(rw_a0[l]), _row(rw_k_k[l]), _row(rw_k_a[l]),
                      _row(rw_r_k[l]), _row(rw_ln_w[l]), _row(rw_ln_b[l]),
                      rw_w_up[l].astype(BF16), rw_a_up[l].astype(BF16), g_up_pad)

        mix = jnp.concatenate([y_fox, y_ml, y_rw], axis=-1)
        x = _matmul_residual(mix, w_out[l].astype(BF16), x, g1, tm=1024, tn=512, tk=d, name="output_projection")

        h = _norm_mod(x, norm2[l], sc2, sh2, BF16)
        pad_ff = D_FF_PAD - D_FF
        wg = jnp.pad(ffn_gate[l].astype(BF16), ((0, 0), (0, pad_ff)))
        wu = jnp.pad(ffn_up[l].astype(BF16), ((0, 0), (0, pad_ff)))
        wd = jnp.pad(ffn_down[l].astype(BF16), ((0, pad_ff), (0, 0)))
        act = _matmul_swiglu(h, wg, wu, tm=1024, tn=512)
        x = _matmul_residual(act, wd, x, g2, tm=1024, tn=512, tk=D_FF_PAD // 4, name="ffn_down")

    zeros = jnp.zeros((batch, 1, d), F32)
    return _norm_mod(x, final_norm, zeros, zeros, F32)
```

```python
import functools

import jax
import jax.numpy as jnp
from jax import lax
from jax.experimental import pallas as pl
from jax.experimental.pallas import tpu as pltpu

F32 = jnp.float32
BF16 = jnp.bfloat16
HIGHEST = lax.Precision.HIGHEST

D_MODEL = 4096
FOX_HEADS, FOX_HEAD_DIM = 12, 128
FOX_WIDTH = FOX_HEADS * FOX_HEAD_DIM
ML_HEADS, ML_QK_DIM, ML_V_DIM = 4, 128, 256
ML_QK_WIDTH = ML_HEADS * ML_QK_DIM
ML_WIDTH = ML_HEADS * ML_V_DIM
ML_CONV = 4
GATE_SOFTCAP = 15.0
RW_HEADS, RW_HEAD_DIM = 24, 64
RW_WIDTH = RW_HEADS * RW_HEAD_DIM
RW_LORA = 128
RW_GATE_LORA = 480
RW_GATE_LORA_PAD = 512
RW_LN_EPS = 64e-5
D_FF = 11008
NORM_EPS = 1e-6

LANE = 128
SUBLANE = 8
VMEM_LIMIT = 56 * 1024 * 1024

_O_FOX_Q, _O_FOX_K, _O_FOX_V, _O_FOX_F = 0, 1536, 3072, 4608
_O_ML = 4620
_O_ML_QK, _O_ML_V, _O_ML_I, _O_ML_F, _O_ML_O = _O_ML, _O_ML + 1024, _O_ML + 2048, _O_ML + 2052, _O_ML + 2056
_O_RW = 7700
_O_RW_R, _O_RW_K, _O_RW_V = _O_RW, _O_RW + 1536, _O_RW + 3072
_O_RW_WL, _O_RW_AL, _O_RW_GL = _O_RW + 4608, _O_RW + 4736, _O_RW + 4864
N_IN = 13044
P_FOX_Q, P_FOX_K, P_FOX_V = 0, 1536, 3072
P_ML_V, P_ML_O, P_ML_Q, P_ML_K = 4608, 5632, 6656, 7168
P_RW_R, P_RW_K, P_RW_V = 7680, 9216, 10752
P_RW_GL, P_RW_WL, P_RW_AL = 12288, 12800, 12928
P_FOX_F, P_ML_IF = 13056, 13184
N_PAD = 13312
_SECTIONS = (
    (P_FOX_Q, _O_FOX_Q, 3 * FOX_WIDTH),
    (P_ML_V, _O_ML_V, ML_WIDTH),
    (P_ML_O, _O_ML_O, ML_WIDTH),
    (P_ML_Q, _O_ML_QK, 2 * ML_QK_WIDTH),
    (P_RW_R, _O_RW_R, 3 * RW_WIDTH),
    (P_RW_GL + RW_GATE_LORA_PAD - RW_GATE_LORA, _O_RW_GL, RW_GATE_LORA),
    (P_RW_WL, _O_RW_WL, 2 * RW_LORA),
    (P_FOX_F, _O_FOX_F, FOX_HEADS),
    (P_ML_IF, _O_ML_I, 2 * ML_HEADS),
)


def _pad_columns(w):
    parts, pos = [], 0
    for p_start, o_start, width in _SECTIONS:
        if p_start > pos:
            parts.append(jnp.zeros(w.shape[:-1] + (p_start - pos,), w.dtype))
        parts.append(w[..., o_start:o_start + width])
        pos = p_start + width
    if pos < N_PAD:
        parts.append(jnp.zeros(w.shape[:-1] + (N_PAD - pos,), w.dtype))
    return jnp.concatenate(parts, axis=-1)


def _params(*sem):
    return pltpu.CompilerParams(dimension_semantics=sem, vmem_limit_bytes=VMEM_LIMIT)


def _repack_table():
    src, lo, hi = [0] * (N_PAD // LANE), [0] * (N_PAD // LANE), [0] * (N_PAD // LANE)
    for p_start, o_start, width in _SECTIONS:
        first = p_start // LANE
        last = (p_start + width - 1) // LANE
        for tile in range(first, last + 1):
            base = tile * LANE
            src[tile] = o_start + (base - p_start)
            lo[tile] = max(p_start - base, 0)
            hi[tile] = min(p_start + width - base, LANE)
            assert 0 <= src[tile] and src[tile] + LANE <= N_IN
    return (jnp.asarray(src, jnp.int32), jnp.asarray(lo, jnp.int32), jnp.asarray(hi, jnp.int32))


def _repack_kernel(src_ref, lo_ref, hi_ref, w_ref, o_ref):
    i = pl.program_id(0)
    row = lax.broadcasted_iota(jnp.int32, (LANE, 1), 0)
    valid = (row >= lo_ref[i]) & (row < hi_ref[i])
    for layer in range(o_ref.shape[0]):
        o_ref[layer] = jnp.where(valid, w_ref[:, layer, :], 0.0).astype(o_ref.dtype)


def _repack_w_in(w_in):
    depth, kdim, _ = w_in.shape
    w_t = jnp.transpose(w_in, (2, 0, 1))
    src, lo, hi = _repack_table()
    return pl.pallas_call(
        _repack_kernel,
        grid_spec=pltpu.PrefetchScalarGridSpec(
            num_scalar_prefetch=3,
            grid=(N_PAD // LANE,),
            in_specs=[pl.BlockSpec((pl.Element(LANE), pl.Element(depth), pl.Element(kdim)),
                                   lambda i, src, lo, hi: (src[i], 0, 0))],
            out_specs=pl.BlockSpec((depth, LANE, kdim), lambda i, src, lo, hi: (0, i, 0)),
        ),
        out_shape=jax.ShapeDtypeStruct((depth, N_PAD, kdim), BF16),
        compiler_params=_params("parallel"),
        name="repack_w_in",
    )(src, lo, hi, w_t)


def _log_sigmoid(x):
    return jnp.minimum(x, 0.0) - jnp.log1p(jnp.exp(-jnp.abs(x)))


def _sigmoid(x):
    return 1.0 / (1.0 + jnp.exp(-x))


def _silu(x):
    return x * _sigmoid(x)


def _dot(a, b, precision=None):
    return jnp.dot(a, b, preferred_element_type=F32, precision=precision)


def _dot_nt(a, b, precision=None):
    return lax.dot_general(a, b, (((1,), (1,)), ((), ())), preferred_element_type=F32, precision=precision)


def _dot_tn(a, b, precision=None):
    return lax.dot_general(a, b, (((0,), (0,)), ((), ())), preferred_element_type=F32, precision=precision)


def _mod_kernel(c_ref, w_ref, b_ref, o_ref):
    a = _silu(c_ref[...]).astype(BF16)
    o_ref[0] = _dot(a, w_ref[0].astype(BF16)) + b_ref[0]


def _modulation(c, ada_w, ada_b):
    depth, d, n = ada_w.shape
    batch = c.shape[0]
    rows = -(-batch // SUBLANE) * SUBLANE
    c_pad = jnp.zeros((rows, d), F32).at[:batch].set(c)
    tn = 512
    out = pl.pallas_call(
        _mod_kernel,
        grid=(depth, n // tn),
        in_specs=[
            pl.BlockSpec((rows, d), lambda l, j: (0, 0)),
            pl.BlockSpec((1, d, tn), lambda l, j: (l, 0, j)),
            pl.BlockSpec((1, 1, tn), lambda l, j: (l, 0, j)),
        ],
        out_specs=pl.BlockSpec((1, rows, tn), lambda l, j: (l, 0, j)),
        out_shape=jax.ShapeDtypeStruct((depth, rows, n), F32),
        compiler_params=_params("parallel", "parallel"),
        name="adaln_modulation",
    )(c_pad, ada_w, ada_b.reshape(depth, 1, n))
    return out[:, :batch]


def _norm_kernel(x_ref, g_ref, sc_ref, sh_ref, o_ref):
    x = x_ref[0]
    y = x * lax.rsqrt(jnp.mean(x * x, axis=-1, keepdims=True) + NORM_EPS)
    y = (y * g_ref[...]) * (1.0 + sc_ref[0]) + sh_ref[0]
    o_ref[0] = y.astype(o_ref.dtype)


def _norm_mod(x, g, scale, shift, out_dtype):
    batch, t, d = x.shape
    tm = min(512, t)
    return pl.pallas_call(
        _norm_kernel,
        grid=(batch, t // tm),
        in_specs=[
            pl.BlockSpec((1, tm, d), lambda b, i: (b, i, 0)),
            pl.BlockSpec((1, d), lambda b, i: (0, 0)),
            pl.BlockSpec((1, 1, d), lambda b, i: (b, 0, 0)),
            pl.BlockSpec((1, 1, d), lambda b, i: (b, 0, 0)),
        ],
        out_specs=pl.BlockSpec((1, tm, d), lambda b, i: (b, i, 0)),
        out_shape=jax.ShapeDtypeStruct((batch, t, d), out_dtype),
        compiler_params=_params("parallel", "parallel"),
        name="rmsnorm_modulate",
    )(x, g.reshape(1, d), scale, shift)


def _mm_plain_kernel(a_ref, wt_ref, o_ref):
    o_ref[0] = _dot_nt(a_ref[0], wt_ref[0]).astype(o_ref.dtype)


def _mm_swiglu_kernel(a_ref, wg_ref, wu_ref, o_ref):
    a = a_ref[0]
    gate = _dot(a, wg_ref[0])
    up = _dot(a, wu_ref[0])
    o_ref[0] = (_silu(gate) * up).astype(o_ref.dtype)


def _mm_residual_kernel(a_ref, w_ref, x_ref, g_ref, o_ref, acc_ref, *, nk):
    k = pl.program_id(3)

    @pl.when(k == 0)
    def _():
        acc_ref[...] = jnp.zeros_like(acc_ref)

    acc_ref[...] += _dot(a_ref[0], w_ref[0])

    @pl.when(k == nk - 1)
    def _():
        o_ref[0] = x_ref[0] + g_ref[0] * acc_ref[...]


def _mm_mix_kernel(yf_ref, ym_ref, yr_ref, w_ref, x_ref, g_ref, o_ref):
    k1, k2 = FOX_WIDTH, FOX_WIDTH + ML_WIDTH
    acc = _dot(yf_ref[0], w_ref[0, :k1, :])
    acc += _dot(ym_ref[0], w_ref[0, k1:k2, :])
    acc += _dot(yr_ref[0], w_ref[0, k2:, :])
    o_ref[0] = x_ref[0] + g_ref[0] * acc


def _cast_kernel(w_ref, o_ref):
    o_ref[...] = w_ref[...].astype(o_ref.dtype)


def _cast_bf16(w, name):
    depth, r, c = w.shape
    tr = 256
    return pl.pallas_call(
        _cast_kernel,
        grid=(depth, r // tr),
        in_specs=[pl.BlockSpec((1, tr, c), lambda l, i: (l, i, 0))],
        out_specs=pl.BlockSpec((1, tr, c), lambda l, i: (l, i, 0)),
        out_shape=jax.ShapeDtypeStruct(w.shape, BF16),
        compiler_params=_params("parallel", "parallel"),
        name=name,
    )(w)


def _matmul_plain(a, w_t, layer, out_dtype, tm, tn):
    batch, t, kdim = a.shape
    n = w_t.shape[1]
    tm = min(tm, t)
    return pl.pallas_call(
        _mm_plain_kernel,
        grid=(batch, t // tm, n // tn),
        in_specs=[
            pl.BlockSpec((1, tm, kdim), lambda b, i, j: (b, i, 0)),
            pl.BlockSpec((1, tn, kdim), lambda b, i, j: (layer, j, 0)),
        ],
        out_specs=pl.BlockSpec((1, tm, tn), lambda b, i, j: (b, i, j)),
        out_shape=jax.ShapeDtypeStruct((batch, t, n), out_dtype),
        compiler_params=_params("parallel", "parallel", "arbitrary"),
        name="input_projection",
    )(a, w_t)


def _matmul_swiglu(a, wg, wu, layer, tm, tn):
    batch, t, kdim = a.shape
    n = wg.shape[2]
    tm = min(tm, t)
    w_spec = pl.BlockSpec((1, kdim, tn), lambda b, i, j: (layer, 0, j))
    return pl.pallas_call(
        _mm_swiglu_kernel,
        grid=(batch, t // tm, pl.cdiv(n, tn)),
        in_specs=[pl.BlockSpec((1, tm, kdim), lambda b, i, j: (b, i, 0)), w_spec, w_spec],
        out_specs=pl.BlockSpec((1, tm, tn), lambda b, i, j: (b, i, j)),
        out_shape=jax.ShapeDtypeStruct((batch, t, n), BF16),
        compiler_params=_params("parallel", "parallel", "arbitrary"),
        name="ffn_gate_up_swiglu",
    )(a, wg, wu)


def _matmul_residual(a, w, layer, x, gate, tm, tn, tk, name):
    batch, t, kdim = a.shape
    n = w.shape[2]
    tm = min(tm, t)
    assert kdim % tk == 0 and tk % LANE == 0
    nk = kdim // tk
    return pl.pallas_call(
        functools.partial(_mm_residual_kernel, nk=nk),
        grid=(batch, t // tm, n // tn, nk),
        in_specs=[
            pl.BlockSpec((1, tm, tk), lambda b, i, j, k: (b, i, k)),
            pl.BlockSpec((1, tk, tn), lambda b, i, j, k: (layer, k, j)),
            pl.BlockSpec((1, tm, tn), lambda b, i, j, k: (b, i, j)),
            pl.BlockSpec((1, 1, tn), lambda b, i, j, k: (b, 0, j)),
        ],
        out_specs=pl.BlockSpec((1, tm, tn), lambda b, i, j, k: (b, i, j)),
        out_shape=jax.ShapeDtypeStruct((batch, t, n), F32),
        scratch_shapes=[pltpu.VMEM((tm, tn), F32)],
        compiler_params=_params("parallel", "parallel", "parallel", "arbitrary"),
        name=name,
    )(a, w, x, gate)


def _matmul_mix(y_fox, y_ml, y_rw, w, layer, x, gate, tm, tn):
    batch, t, _ = x.shape
    kdim, n = w.shape[1], w.shape[2]
    tm = min(tm, t)
    act = lambda width: pl.BlockSpec((1, tm, width), lambda b, i, j: (b, i, 0))
    return pl.pallas_call(
        _mm_mix_kernel,
        grid=(batch, t // tm, n // tn),
        in_specs=[
            act(FOX_WIDTH), act(ML_WIDTH), act(RW_WIDTH),
            pl.BlockSpec((1, kdim, tn), lambda b, i, j: (layer, 0, j)),
            pl.BlockSpec((1, tm, tn), lambda b, i, j: (b, i, j)),
            pl.BlockSpec((1, 1, tn), lambda b, i, j: (b, 0, j)),
        ],
        out_specs=pl.BlockSpec((1, tm, tn), lambda b, i, j: (b, i, j)),
        out_shape=jax.ShapeDtypeStruct((batch, t, n), F32),
        compiler_params=_params("parallel", "parallel", "arbitrary"),
        name="output_projection",
    )(y_fox, y_ml, y_rw, w, x, gate)


ML_CHUNK = 128
_CUM_ROWS = 256


def _gates_kernel(ff_ref, mif_ref, fb_ref, mb_ref, cum_ref, li_ref, bcs_ref, *, t):
    r = _CUM_ROWS if t % _CUM_ROWS == 0 else t
    row = lax.broadcasted_iota(jnp.int32, (r, r), 0)
    col = lax.broadcasted_iota(jnp.int32, (r, r), 1)
    tril = (col <= row).astype(F32)
    fb = fb_ref[...]

    def cum_body(i, carry):
        start = pl.multiple_of(i * r, r)
        lg = _log_sigmoid(ff_ref[0, pl.ds(start, r), :] + fb)
        cs = _dot(tril, lg, HIGHEST) + carry
        cum_ref[0, pl.ds(start, r), :] = cs
        return cs[r - 1:r, :]

    lax.fori_loop(0, t // r, cum_body, jnp.zeros((1, LANE), F32))

    lc = ML_CHUNK
    rowc = lax.broadcasted_iota(jnp.int32, (lc, lc), 0)
    colc = lax.broadcasted_iota(jnp.int32, (lc, lc), 1)
    trilc = (colc <= rowc).astype(F32)
    mb = mb_ref[...]

    def ml_body(i, carry):
        start = pl.multiple_of(i * lc, lc)
        z = mif_ref[0, pl.ds(start, lc), :] + mb
        capped = GATE_SOFTCAP * jnp.tanh(z / GATE_SOFTCAP)
        li_ref[0, pl.ds(start, lc), :] = capped
        bcs_ref[0, pl.ds(start, lc), :] = _dot(trilc, _log_sigmoid(capped), HIGHEST)
        return carry

    lax.fori_loop(0, t // lc, ml_body, 0)


def _gates(p, fox_bias_row, ml_bias_row):
    batch, t, _ = p.shape
    blk = lambda col: pl.BlockSpec((1, t, LANE), lambda b: (b, 0, col // LANE))
    row = pl.BlockSpec((1, LANE), lambda b: (0, 0))
    out = pl.BlockSpec((1, t, LANE), lambda b: (b, 0, 0))
    shp = jax.ShapeDtypeStruct((batch, t, LANE), F32)
    return pl.pallas_call(
        functools.partial(_gates_kernel, t=t),
        grid=(batch,),
        in_specs=[blk(P_FOX_F), blk(P_ML_IF), row, row],
        out_specs=[out, out, out],
        out_shape=[shp, shp, shp],
        compiler_params=_params("parallel"),
        name="gate_cumsums",
    )(p, p, fox_bias_row, ml_bias_row)


FOX_TQ = 512
FOX_TK = 512


def _fox_kernel(q_ref, k_ref, v_ref, fq_ref, fk_ref, g_ref, o_ref, *, tq, tk):
    i = pl.program_id(2)
    scale = FOX_HEAD_DIM ** -0.5
    q = (q_ref[0] * scale).astype(BF16)
    fq = fq_ref[0, 0]
    q_pos = i * tq + lax.broadcasted_iota(jnp.int32, (tq, 1), 0)

    def step(j, carry, masked):
        m, l, acc = carry
        start = pl.multiple_of(j * tk, tk)
        ks = k_ref[0, pl.ds(start, tk), :].astype(BF16)
        vs = v_ref[0, pl.ds(start, tk), :].astype(BF16)
        fk = fk_ref[0, 0, :, pl.ds(start, tk)]
        s = (_dot_nt(q, ks) + fq) - fk
        if masked:
            k_pos = start + lax.broadcasted_iota(jnp.int32, (1, tk), 1)
            s = jnp.where(k_pos <= q_pos, s, -jnp.inf)
        m_new = jnp.maximum(m, jnp.max(s, axis=-1, keepdims=True))
        p = jnp.exp(s - m_new)
        alpha = jnp.exp(m - m_new)
        l = alpha * l + jnp.sum(p, axis=-1, keepdims=True)
        acc = alpha * acc + _dot(p.astype(BF16), vs)
        return m_new, l, acc

    per_tile = tq // tk
    carry = (jnp.full((tq, 1), -1e30, F32), jnp.zeros((tq, 1), F32), jnp.zeros((tq, FOX_HEAD_DIM), F32))
    carry = lax.fori_loop(0, i * per_tile, functools.partial(step, masked=False), carry)
    _, l, acc = lax.fori_loop(i * per_tile, (i + 1) * per_tile, functools.partial(step, masked=True), carry)
    o = acc / l
    o = o * lax.rsqrt(jnp.mean(o * o, axis=-1, keepdims=True) + NORM_EPS)
    o_ref[0] = (o * g_ref[...]).astype(o_ref.dtype)


def _fox_attention(p, cum_col, cum_row, out_g):
    batch, t, _ = p.shape
    tq = min(FOX_TQ, t)
    tk = min(FOX_TK, tq)
    dh = FOX_HEAD_DIM
    return pl.pallas_call(
        functools.partial(_fox_kernel, tq=tq, tk=tk),
        grid=(batch, FOX_HEADS, t // tq),
        in_specs=[
            pl.BlockSpec((1, tq, dh), lambda b, h, i: (b, i, P_FOX_Q // dh + h)),
            pl.BlockSpec((1, t, dh), lambda b, h, i: (b, 0, P_FOX_K // dh + h)),
            pl.BlockSpec((1, t, dh), lambda b, h, i: (b, 0, P_FOX_V // dh + h)),
            pl.BlockSpec((1, 1, tq, 1), lambda b, h, i: (b, h, i, 0)),
            pl.BlockSpec((1, 1, 1, t), lambda b, h, i: (b, h, 0, 0)),
            pl.BlockSpec((1, dh), lambda b, h, i: (0, h)),
        ],
        out_specs=pl.BlockSpec((1, tq, dh), lambda b, h, i: (b, i, h)),
        out_shape=jax.ShapeDtypeStruct((batch, t, FOX_WIDTH), BF16),
        compiler_params=_params("parallel", "parallel", "arbitrary"),
        name="fox_attention",
    )(p, p, p, cum_col, cum_row, out_g)


def _mlstm_kernel(q_ref, k_ref, v_ref, o_ref, grow_ref, gcol_ref, cwq_ref, cwk_ref, cbq_ref, cbk_ref,
                  ng_ref, y_ref, *, t):
    lc = ML_CHUNK
    dk, dv = ML_QK_DIM, ML_V_DIM
    row = lax.broadcasted_iota(jnp.int32, (lc, lc), 0)
    col = lax.broadcasted_iota(jnp.int32, (lc, lc), 1)
    causal = col <= row
    cwq, cwk = cwq_ref[...], cwk_ref[...]
    cbq, cbk = cbq_ref[...], cbk_ref[...]
    norm_g = ng_ref[...]

    def conv_silu(ref, start, prev_ok, w, b):
        cur = ref[0, pl.ds(start, lc), :]
        prev = ref[0, pl.ds(jnp.maximum(start - SUBLANE, 0), SUBLANE), :]
        prev = jnp.where(prev_ok, prev, 0.0)
        ext = jnp.concatenate([prev, cur], axis=0)
        out = b
        for j in range(ML_CONV):
            off = SUBLANE - (ML_CONV - 1) + j
            out = out + ext[off:off + lc, :] * w[j:j + 1, :]
        return _silu(out)

    def body(c, carry):
        cmat, nvec, m = carry
        start = pl.multiple_of(c * lc, lc)
        prev_ok = c > 0
        qt = conv_silu(q_ref, start, prev_ok, cwq, cbq) * (dk ** -0.5)
        kt = conv_silu(k_ref, start, prev_ok, cwk, cbk)
        vt = v_ref[0, pl.ds(start, lc), :]
        li_row = grow_ref[0, 0, 0:1, pl.ds(start, lc)]
        b_row = grow_ref[0, 0, 1:2, pl.ds(start, lc)]
        li_col = gcol_ref[0, 0, pl.ds(start, lc), 0:1]
        b_col = gcol_ref[0, 0, pl.ds(start, lc), 1:2]
        g = b_row[:, lc - 1:lc]

        a_inter = b_col + m
        dm = jnp.where(causal, b_col - b_row + li_row, -jnp.inf)
        m_t = jnp.maximum(a_inter, jnp.max(dm, axis=-1, keepdims=True))
        w_inter = jnp.exp(a_inter - m_t)
        qb, kb = qt.astype(BF16), kt.astype(BF16)
        s = _dot_nt(qb, kb) * jnp.exp(dm - m_t)
        num = w_inter * _dot(qb, cmat.astype(BF16)) + _dot(s.astype(BF16), vt.astype(BF16))
        den = w_inter * jnp.sum(qt * nvec, axis=-1, keepdims=True) + jnp.sum(s, axis=-1, keepdims=True)
        h = num / jnp.maximum(jnp.abs(den), jnp.exp(-m_t))

        upd = g - b_col + li_col
        m_new = jnp.maximum(g + m, jnp.max(upd, axis=0, keepdims=True))
        decay = jnp.exp(g + m - m_new)
        wk = jnp.exp(upd - m_new) * kt
        cmat = decay * cmat + _dot_tn(wk.astype(BF16), vt.astype(BF16))
        nvec = decay * nvec + jnp.sum(wk, axis=0, keepdims=True)

        hn = h * lax.rsqrt(jnp.mean(h * h, axis=-1, keepdims=True) + NORM_EPS) * norm_g
        og = o_ref[0, pl.ds(start, lc), :]
        y_ref[0, pl.ds(start, lc), :] = (hn * _sigmoid(og)).astype(y_ref.dtype)
        return cmat, nvec, m_new

    init = (jnp.zeros((dk, dv), F32), jnp.zeros((1, dk), F32), jnp.zeros((1, 1), F32))
    lax.fori_loop(0, t // lc, body, init)


def _mlstm(p, gate_rows, gate_cols, conv_w, conv_b, norm_g):
    batch, t, _ = p.shape
    dk, dv = ML_QK_DIM, ML_V_DIM
    nh = ML_HEADS
    return pl.pallas_call(
        functools.partial(_mlstm_kernel, t=t),
        grid=(batch, nh),
        in_specs=[
            pl.BlockSpec((1, t, dk), lambda b, h: (b, 0, P_ML_Q // dk + h)),
            pl.BlockSpec((1, t, dk), lambda b, h: (b, 0, P_ML_K // dk + h)),
            pl.BlockSpec((1, t, dv), lambda b, h: (b, 0, P_ML_V // dv + h)),
            pl.BlockSpec((1, t, dv), lambda b, h: (b, 0, P_ML_O // dv + h)),
            pl.BlockSpec((1, 1, 2, t), lambda b, h: (b, h, 0, 0)),
            pl.BlockSpec((1, 1, t, 2), lambda b, h: (b, h, 0, 0)),
            pl.BlockSpec((ML_CONV, dk), lambda b, h: (0, h)),
            pl.BlockSpec((ML_CONV, dk), lambda b, h: (0, nh + h)),
            pl.BlockSpec((1, dk), lambda b, h: (0, h)),
            pl.BlockSpec((1, dk), lambda b, h: (0, nh + h)),
            pl.BlockSpec((1, dv), lambda b, h: (0, h)),
        ],
        out_specs=pl.BlockSpec((1, t, dv), lambda b, h: (b, 0, h)),
        out_shape=jax.ShapeDtypeStruct((batch, t, ML_WIDTH), BF16),
        compiler_params=_params("parallel", "parallel"),
        name="mlstm_chunkwise",
    )(p, p, p, p, gate_rows, gate_cols, conv_w, conv_w, conv_b, conv_b, norm_g)


RW_CHUNK = 64
RW_TILE = 512


def _rw_kernel(r_ref, k_ref, v_ref, wl_ref, al_ref, gl_ref,
               rp_ref, kp_ref, vp_ref, wlp_ref, alp_ref, glp_ref,
               mur_ref, muk_ref, muv_ref, muw_ref, mua_ref, mug_ref,
               w0_ref, a0_ref, kkw_ref, kaw_ref, rkw_ref, lnw_ref, lnb_ref,
               wup_ref, aup_ref, gup_ref, y_ref, s_ref, *, tm):
    i = pl.program_id(2)
    lc = RW_CHUNK
    n = RW_HEAD_DIM
    pair = LANE

    @pl.when(i == 0)
    def _():
        s_ref[...] = jnp.zeros_like(s_ref)

    first = lax.broadcasted_iota(jnp.int32, (tm, 1), 0) == 0
    has_prev = i > 0

    def shift(cur_ref, prev_ref, mu_ref):
        cur = cur_ref[0]
        last = jnp.where(has_prev, prev_ref[0][SUBLANE - 1:SUBLANE, :], 0.0)
        prev = jnp.where(first, last, pltpu.roll(cur, 1, axis=0))
        return cur + (prev - cur) * mu_ref[...]

    r = shift(r_ref, rp_ref, mur_ref)
    k = shift(k_ref, kp_ref, muk_ref)
    v = shift(v_ref, vp_ref, muv_ref)
    wl = shift(wl_ref, wlp_ref, muw_ref)
    al = shift(al_ref, alp_ref, mua_ref)
    gl = shift(gl_ref, glp_ref, mug_ref)

    hrow = lax.broadcasted_iota(jnp.int32, (pair, pair), 0)
    hcol = lax.broadcasted_iota(jnp.int32, (pair, pair), 1)
    same_head = jnp.where(hrow // n == hcol // n, 1.0, 0.0).astype(BF16)

    def head_sum(x):
        hi = x.astype(BF16)
        lo = (x - hi.astype(F32)).astype(BF16)
        return _dot(hi, same_head) + _dot(lo, same_head)

    z = w0_ref[...] + _dot(jnp.tanh(wl).astype(BF16), wup_ref[...])
    lw = -jnp.exp(_log_sigmoid(z) - 0.5)
    a = _sigmoid(a0_ref[...] + _dot(al.astype(BF16), aup_ref[...]))
    gate = _dot(_sigmoid(gl).astype(BF16), gup_ref[...])
    kk = k * kkw_ref[...]
    kk = kk / jnp.maximum(jnp.sqrt(head_sum(kk * kk)), 1e-12)
    k = k * (1.0 + (a - 1.0) * kaw_ref[...])
    bb = kk * a
    bonus = head_sum(r * k * rkw_ref[...]) * v

    trow = lax.broadcasted_iota(jnp.int32, (lc, lc), 0)
    tcol = lax.broadcasted_iota(jnp.int32, (lc, lc), 1)
    tri = jnp.where(tcol <= trow, 1.0, 0.0).astype(BF16)
    lw_hi = lw.astype(BF16)
    rem = lw - lw_hi.astype(F32)
    lw_mid = rem.astype(BF16)
    lw_lo = (rem - lw_mid.astype(F32)).astype(BF16)
    nchunk = tm // lc
    cums, ends = [], []
    for c in range(nchunk):
        sl = slice(c * lc, (c + 1) * lc)
        cum_c = _dot(tri, lw_hi[sl]) + (_dot(tri, lw_mid[sl]) + _dot(tri, lw_lo[sl]))
        cums.append(cum_c)
        ends.append(jnp.broadcast_to(cum_c[lc - 1:lc, :], (lc, LANE)))
    cum = jnp.concatenate(cums, axis=0)
    cum_end = jnp.concatenate(ends, axis=0)
    e_inc = jnp.exp(cum)
    e_neg = jnp.exp(-cum)
    tail = jnp.exp(cum_end - cum)
    rt = r * e_inc
    at = kk * jnp.exp(cum - lw)
    kt = k * e_neg
    bt = bb * e_neg
    kh = k * tail
    bh = bb * tail

    head0 = lax.broadcasted_iota(jnp.int32, (1, LANE), 1) < n

    def stack(x, dtype=BF16):
        return jnp.concatenate([jnp.where(head0, x, 0.0), jnp.where(head0, 0.0, x)], axis=0).astype(dtype)

    prow = lax.broadcasted_iota(jnp.int32, (pair, pair), 0)
    pcol = lax.broadcasted_iota(jnp.int32, (pair, pair), 1)
    lower_strict = pcol < prow
    lower_incl = pcol <= prow
    eye = jnp.where(pcol == prow, 1.0, 0.0)

    chunks = range(nchunk)
    rows = [slice(c * lc, (c + 1) * lc) for c in chunks]
    a2 = [stack(at[sl]) for sl in rows]
    r2 = [stack(rt[sl], F32) for sl in rows]
    v2 = [stack(v[sl]) for sl in rows]
    mm = [_dot_nt(jnp.concatenate([a2[c], r2[c].astype(BF16)], axis=0),
                  jnp.concatenate([stack(bt[rows[c]]), stack(kt[rows[c]])], axis=0)) for c in chunks]
    m_ak = [jnp.where(lower_strict, mm[c][:pair, pair:], 0.0).astype(BF16) for c in chunks]
    m_rb = [jnp.where(lower_incl, mm[c][pair:, :pair], 0.0).astype(BF16) for c in chunks]
    m_rk = [jnp.where(lower_incl, mm[c][pair:, pair:], 0.0).astype(BF16) for c in chunks]

    m_ab = [jnp.where(lower_strict, mm[c][:pair, :pair], 0.0) for c in chunks]
    i_plus_m = [(eye + m_ab[c]).astype(BF16) for c in chunks]
    t_inv = [eye - m_ab[c] for c in chunks]
    for _ in range((lc - 1).bit_length() - 1):
        res = [_dot(i_plus_m[c], t_inv[c].astype(BF16)) for c in chunks]
        t_inv = [_dot(t_inv[c].astype(BF16), (2.0 * eye - res[c]).astype(BF16)) for c in chunks]

    x1 = [_dot(m_ak[c], v2[c]) for c in chunks]
    aub = [_dot(t_inv[c].astype(BF16), jnp.concatenate([a2[c], x1[c].astype(BF16)], axis=1)).astype(BF16)
           for c in chunks]
    rb = [_dot(m_rb[c], aub[c]) for c in chunks]
    r_eff = [r2[c] - rb[c][:, :pair] for c in chunks]
    y_loc = [_dot(m_rk[c], v2[c]) - rb[c][:, pair:] for c in chunks]
    r_eff = [(r_eff[c][:lc] + r_eff[c][lc:]).astype(BF16) for c in chunks]
    y_loc = [y_loc[c][:lc] + y_loc[c][lc:] for c in chunks]
    gd = [_dot_tn(aub[c], stack(bh[rows[c]])) for c in chunks]
    g_mat = [(eye * e_inc[(c + 1) * lc - 1:(c + 1) * lc, :] - gd[c][:pair]).astype(BF16) for c in chunks]
    d_mat = [_dot_tn(v2[c], stack(kh[rows[c]])) - gd[c][pair:] for c in chunks]

    s_mat = s_ref[...]
    ys = []
    for c in chunks:
        s_b = s_mat.astype(BF16)
        ys.append(_dot_nt(r_eff[c], s_b) + y_loc[c])
        s_mat = _dot(s_b, g_mat[c]) + d_mat[c]
    s_ref[...] = s_mat

    y = jnp.concatenate(ys, axis=0)
    mean = head_sum(y) * (1.0 / n)
    cen = y - mean
    var = head_sum(cen * cen) * (1.0 / n)
    y = cen * lax.rsqrt(var + RW_LN_EPS) * lnw_ref[...] + lnb_ref[...]
    y_ref[0] = ((y + bonus) * gate).astype(y_ref.dtype)


def _rwkv7(p, mu_pad, w0, a0, k_k, k_a, r_k, ln_w, ln_b, w_up, a_up, g_up_pad):
    batch, t, _ = p.shape
    tm = min(RW_TILE, t)
    nj = RW_WIDTH // LANE
    gw = RW_GATE_LORA_PAD
    rb = tm // SUBLANE

    def cur(col, width=LANE, tiled=True):
        base = col // width
        if tiled:
            return pl.BlockSpec((1, tm, width), lambda b, j, i: (b, i, base + j))
        return pl.BlockSpec((1, tm, width), lambda b, j, i: (b, i, base))

    def prev(col, width=LANE, tiled=True):
        base = col // width
        if tiled:
            return pl.BlockSpec((1, SUBLANE, width), lambda b, j, i: (b, jnp.maximum(i * rb - 1, 0), base + j))
        return pl.BlockSpec((1, SUBLANE, width), lambda b, j, i: (b, jnp.maximum(i * rb - 1, 0), base))

    def murow(col, width=LANE, tiled=True):
        base = col // width
        if tiled:
            return pl.BlockSpec((1, width), lambda b, j, i: (0, base + j))
        return pl.BlockSpec((1, width), lambda b, j, i: (0, base))

    vec = pl.BlockSpec((1, LANE), lambda b, j, i: (0, j))
    up = pl.BlockSpec((RW_LORA, LANE), lambda b, j, i: (0, j))
    gup = pl.BlockSpec((gw, LANE), lambda b, j, i: (0, j))
    return pl.pallas_call(
        functools.partial(_rw_kernel, tm=tm),
        grid=(batch, nj, t // tm),
        in_specs=[
            cur(P_RW_R), cur(P_RW_K), cur(P_RW_V),
            cur(P_RW_WL, tiled=False), cur(P_RW_AL, tiled=False), cur(P_RW_GL, gw, tiled=False),
            prev(P_RW_R), prev(P_RW_K), prev(P_RW_V),
            prev(P_RW_WL, tiled=False), prev(P_RW_AL, tiled=False), prev(P_RW_GL, gw, tiled=False),
            murow(P_RW_R), murow(P_RW_K), murow(P_RW_V),
            murow(P_RW_WL, tiled=False), murow(P_RW_AL, tiled=False), murow(P_RW_GL, gw, tiled=False),
            vec, vec, vec, vec, vec, vec, vec, up, up, gup,
        ],
        out_specs=pl.BlockSpec((1, tm, LANE), lambda b, j, i: (b, i, j)),
        out_shape=jax.ShapeDtypeStruct((batch, t, RW_WIDTH), BF16),
        scratch_shapes=[pltpu.VMEM((LANE, LANE), F32)],
        compiler_params=_params("parallel", "parallel", "arbitrary"),
        name="rwkv7_chunkwise",
    )(p, p, p, p, p, p, p, p, p, p, p, p,
      mu_pad, mu_pad, mu_pad, mu_pad, mu_pad, mu_pad,
      w0, a0, k_k, k_a, r_k, ln_w, ln_b, w_up, a_up, g_up_pad)


def _row(v, width=None):
    v = v.reshape(1, -1).astype(F32)
    if width is not None and v.shape[1] < width:
        v = jnp.pad(v, ((0, 0), (0, width - v.shape[1])))
    return v


def kernel(x, c, ada_w, ada_b, norm1, w_in, fox_f_bias, fox_norm, ml_conv_w, ml_conv_b, ml_i_bias, ml_f_bias,
           ml_norm, rw_mu, rw_w0, rw_w_up, rw_a0, rw_a_up, rw_g_up, rw_k_k, rw_k_a, rw_r_k, rw_ln_w, rw_ln_b,
           w_out, norm2, ffn_gate, ffn_up, ffn_down, final_norm):
    batch, t, d = x.shape
    depth = ada_w.shape[0]
    assert d == D_MODEL and t % ML_CHUNK == 0 and t % RW_CHUNK == 0

    mod = _modulation(c, ada_w, ada_b)
    mod = mod.reshape(depth, batch, 6, 1, d)
    w_in_t = _repack_w_in(w_in)
    w_out_b = _cast_bf16(w_out, "cast_w_out")
    w_gate_b = _cast_bf16(ffn_gate, "cast_ffn_gate")
    w_up_b = _cast_bf16(ffn_up, "cast_ffn_up")
    w_down_b = _cast_bf16(ffn_down, "cast_ffn_down")

    for l in range(depth):
        sh1, sc1, g1, sh2, sc2, g2 = (mod[l, :, i] for i in range(6))

        h = _norm_mod(x, norm1[l], sc1, sh1, BF16)
        p = _matmul_plain(h, w_in_t, l, F32, tm=1024, tn=512)

        fox_bias_row = _row(fox_f_bias[l], LANE)
        ml_bias_row = _row(jnp.concatenate([ml_i_bias[l], ml_f_bias[l]]), LANE)
        cum, li_all, bcs_all = _gates(p, fox_bias_row, ml_bias_row)
        cum_h = cum[:, :, :FOX_HEADS].transpose(0, 2, 1)
        y_fox = _fox_attention(p, cum_h[..., None], cum_h[:, :, None, :], _row(fox_norm[l]))

        li = li_all[:, :, :ML_HEADS]
        bcs = bcs_all[:, :, ML_HEADS:2 * ML_HEADS]
        gate_cols = jnp.stack([li, bcs], axis=-1).transpose(0, 2, 1, 3)
        gate_rows = gate_cols.transpose(0, 1, 3, 2)
        y_ml = _mlstm(p, gate_rows, gate_cols, ml_conv_w[l], _row(ml_conv_b[l]), _row(ml_norm[l]))

        mu_full = jnp.zeros((1, N_IN), F32).at[:, _O_RW:].set(rw_mu[l][None, :])
        mu_pad = _pad_columns(mu_full)
        g_up_pad = jnp.pad(rw_g_up[l], ((RW_GATE_LORA_PAD - RW_GATE_LORA, 0), (0, 0))).astype(BF16)
        y_rw = _rwkv7(p, mu_pad, _row(rw_w0[l]), _row(rw_a0[l]), _row(rw_k_k[l]), _row(rw_k_a[l]),
                      _row(rw_r_k[l]), _row(rw_ln_w[l]), _row(rw_ln_b[l]),
                      rw_w_up[l].astype(BF16), rw_a_up[l].astype(BF16), g_up_pad)

        x = _matmul_mix(y_fox, y_ml, y_rw, w_out_b, l, x, g1, tm=1024, tn=512)

        h = _norm_mod(x, norm2[l], sc2, sh2, BF16)
        act = _matmul_swiglu(h, w_gate_b, w_up_b, l, tm=1024, tn=512)
        x = _matmul_residual(act, w_down_b, l, x, g2, tm=1024, tn=512, tk=D_FF // 2, name="ffn_down")

    zeros = jnp.zeros((batch, 1, d), F32)
    return _norm_mod(x, final_norm, zeros, zeros, F32)
```

```python
import functools

import jax
import jax.numpy as jnp
from jax import lax
from jax.experimental import pallas as pl
from jax.experimental.pallas import tpu as pltpu

F32 = jnp.float32
BF16 = jnp.bfloat16
HIGHEST = lax.Precision.HIGHEST

D_MODEL = 4096
FOX_HEADS, FOX_HEAD_DIM = 12, 128
FOX_WIDTH = FOX_HEADS * FOX_HEAD_DIM
ML_HEADS, ML_QK_DIM, ML_V_DIM = 4, 128, 256
ML_QK_WIDTH = ML_HEADS * ML_QK_DIM
ML_WIDTH = ML_HEADS * ML_V_DIM
ML_CONV = 4
GATE_SOFTCAP = 15.0
RW_HEADS, RW_HEAD_DIM = 24, 64
RW_WIDTH = RW_HEADS * RW_HEAD_DIM
RW_LORA = 128
RW_GATE_LORA = 480
RW_GATE_LORA_PAD = 512
RW_LN_EPS = 64e-5
D_FF = 11008
NORM_EPS = 1e-6
LOG2_E = 1.4426950408889634

LANE = 128
SUBLANE = 8
VMEM_LIMIT = 56 * 1024 * 1024

_O_FOX_Q, _O_FOX_K, _O_FOX_V, _O_FOX_F = 0, 1536, 3072, 4608
_O_ML = 4620
_O_ML_QK, _O_ML_V, _O_ML_I, _O_ML_F, _O_ML_O = _O_ML, _O_ML + 1024, _O_ML + 2048, _O_ML + 2052, _O_ML + 2056
_O_RW = 7700
_O_RW_R, _O_RW_K, _O_RW_V = _O_RW, _O_RW + 1536, _O_RW + 3072
_O_RW_WL, _O_RW_AL, _O_RW_GL = _O_RW + 4608, _O_RW + 4736, _O_RW + 4864
N_IN = 13044
P_FOX_Q, P_FOX_K, P_FOX_V = 0, 1536, 3072
P_ML_V, P_ML_O, P_ML_Q, P_ML_K = 4608, 5632, 6656, 7168
P_RW_R, P_RW_K, P_RW_V = 7680, 9216, 10752
P_RW_GL, P_RW_WL, P_RW_AL = 12288, 12800, 12928
P_FOX_F, P_ML_IF = 13056, 13184
N_PAD = 13312
_SECTIONS = (
    (P_FOX_Q, _O_FOX_Q, 3 * FOX_WIDTH),
    (P_ML_V, _O_ML_V, ML_WIDTH),
    (P_ML_O, _O_ML_O, ML_WIDTH),
    (P_ML_Q, _O_ML_QK, 2 * ML_QK_WIDTH),
    (P_RW_R, _O_RW_R, 3 * RW_WIDTH),
    (P_RW_GL + RW_GATE_LORA_PAD - RW_GATE_LORA, _O_RW_GL, RW_GATE_LORA),
    (P_RW_WL, _O_RW_WL, 2 * RW_LORA),
    (P_FOX_F, _O_FOX_F, FOX_HEADS),
    (P_ML_IF, _O_ML_I, 2 * ML_HEADS),
)


def _pad_columns(w):
    parts, pos = [], 0
    for p_start, o_start, width in _SECTIONS:
        if p_start > pos:
            parts.append(jnp.zeros(w.shape[:-1] + (p_start - pos,), w.dtype))
        parts.append(w[..., o_start:o_start + width])
        pos = p_start + width
    if pos < N_PAD:
        parts.append(jnp.zeros(w.shape[:-1] + (N_PAD - pos,), w.dtype))
    return jnp.concatenate(parts, axis=-1)


def _params(*sem):
    return pltpu.CompilerParams(dimension_semantics=sem, vmem_limit_bytes=VMEM_LIMIT)


def _repack_table():
    src, lo, hi = [0] * (N_PAD // LANE), [0] * (N_PAD // LANE), [0] * (N_PAD // LANE)
    for p_start, o_start, width in _SECTIONS:
        first = p_start // LANE
        last = (p_start + width - 1) // LANE
        for tile in range(first, last + 1):
            base = tile * LANE
            src[tile] = o_start + (base - p_start)
            lo[tile] = max(p_start - base, 0)
            hi[tile] = min(p_start + width - base, LANE)
            assert 0 <= src[tile] and src[tile] + LANE <= N_IN
    return (jnp.asarray(src, jnp.int32), jnp.asarray(lo, jnp.int32), jnp.asarray(hi, jnp.int32))


def _repack_kernel(src_ref, lo_ref, hi_ref, w_ref, o_ref):
    i = pl.program_id(0)
    row = lax.broadcasted_iota(jnp.int32, (LANE, 1), 0)
    valid = (row >= lo_ref[i]) & (row < hi_ref[i])
    for layer in range(o_ref.shape[0]):
        o_ref[layer] = jnp.where(valid, w_ref[:, layer, :], 0.0).astype(o_ref.dtype)


def _repack_w_in(w_in):
    depth, kdim, _ = w_in.shape
    w_t = jnp.transpose(w_in, (2, 0, 1))
    src, lo, hi = _repack_table()
    return pl.pallas_call(
        _repack_kernel,
        grid_spec=pltpu.PrefetchScalarGridSpec(
            num_scalar_prefetch=3,
            grid=(N_PAD // LANE,),
            in_specs=[pl.BlockSpec((pl.Element(LANE), pl.Element(depth), pl.Element(kdim)),
                                   lambda i, src, lo, hi: (src[i], 0, 0))],
            out_specs=pl.BlockSpec((depth, LANE, kdim), lambda i, src, lo, hi: (0, i, 0)),
        ),
        out_shape=jax.ShapeDtypeStruct((depth, N_PAD, kdim), BF16),
        compiler_params=_params("parallel"),
        name="repack_w_in",
    )(src, lo, hi, w_t)


def _log_sigmoid(x):
    return jnp.minimum(x, 0.0) - jnp.log1p(jnp.exp(-jnp.abs(x)))


def _sigmoid(x):
    return 1.0 / (1.0 + jnp.exp(-x))


def _silu(x):
    return x * _sigmoid(x)


def _dot(a, b, precision=None):
    return jnp.dot(a, b, preferred_element_type=F32, precision=precision)


def _dot_nt(a, b, precision=None):
    return lax.dot_general(a, b, (((1,), (1,)), ((), ())), preferred_element_type=F32, precision=precision)


def _dot_tn(a, b, precision=None):
    return lax.dot_general(a, b, (((0,), (0,)), ((), ())), preferred_element_type=F32, precision=precision)


def _mod_kernel(c_ref, w_ref, b_ref, o_ref):
    a = _silu(c_ref[...]).astype(BF16)
    o_ref[0] = _dot(a, w_ref[0].astype(BF16)) + b_ref[0]


def _modulation(c, ada_w, ada_b):
    depth, d, n = ada_w.shape
    batch = c.shape[0]
    rows = -(-batch // SUBLANE) * SUBLANE
    c_pad = jnp.zeros((rows, d), F32).at[:batch].set(c)
    tn = 512
    out = pl.pallas_call(
        _mod_kernel,
        grid=(depth, n // tn),
        in_specs=[
            pl.BlockSpec((rows, d), lambda l, j: (0, 0)),
            pl.BlockSpec((1, d, tn), lambda l, j: (l, 0, j)),
            pl.BlockSpec((1, 1, tn), lambda l, j: (l, 0, j)),
        ],
        out_specs=pl.BlockSpec((1, rows, tn), lambda l, j: (l, 0, j)),
        out_shape=jax.ShapeDtypeStruct((depth, rows, n), F32),
        compiler_params=_params("parallel", "parallel"),
        name="adaln_modulation",
    )(c_pad, ada_w, ada_b.reshape(depth, 1, n))
    return out[:, :batch]


def _norm_kernel(x_ref, g_ref, sc_ref, sh_ref, o_ref):
    x = x_ref[0]
    y = x * lax.rsqrt(jnp.mean(x * x, axis=-1, keepdims=True) + NORM_EPS)
    y = (y * g_ref[...]) * (1.0 + sc_ref[0]) + sh_ref[0]
    o_ref[0] = y.astype(o_ref.dtype)


def _norm_mod(x, g, scale, shift, out_dtype):
    batch, t, d = x.shape
    tm = min(512, t)
    return pl.pallas_call(
        _norm_kernel,
        grid=(batch, t // tm),
        in_specs=[
            pl.BlockSpec((1, tm, d), lambda b, i: (b, i, 0)),
            pl.BlockSpec((1, d), lambda b, i: (0, 0)),
            pl.BlockSpec((1, 1, d), lambda b, i: (b, 0, 0)),
            pl.BlockSpec((1, 1, d), lambda b, i: (b, 0, 0)),
        ],
        out_specs=pl.BlockSpec((1, tm, d), lambda b, i: (b, i, 0)),
        out_shape=jax.ShapeDtypeStruct((batch, t, d), out_dtype),
        compiler_params=_params("parallel", "parallel"),
        name="rmsnorm_modulate",
    )(x, g.reshape(1, d), scale, shift)


def _mm_plain_kernel(a_ref, wt_ref, o_ref):
    o_ref[0] = _dot_nt(a_ref[0], wt_ref[0]).astype(o_ref.dtype)


def _mm_swiglu_kernel(a_ref, wg_ref, wu_ref, o_ref):
    a = a_ref[0]
    gate = _dot(a, wg_ref[0])
    up = _dot(a, wu_ref[0])
    o_ref[0] = (_silu(gate) * up).astype(o_ref.dtype)


def _mm_residual_kernel(a_ref, w_ref, x_ref, g_ref, o_ref, acc_ref, *, nk):
    k = pl.program_id(3)

    @pl.when(k == 0)
    def _():
        acc_ref[...] = jnp.zeros_like(acc_ref)

    acc_ref[...] += _dot(a_ref[0], w_ref[0])

    @pl.when(k == nk - 1)
    def _():
        o_ref[0] = x_ref[0] + g_ref[0] * acc_ref[...]


def _mm_mix_kernel(yf_ref, ym_ref, yr_ref, w_ref, x_ref, g_ref, o_ref):
    k1, k2 = FOX_WIDTH, FOX_WIDTH + ML_WIDTH
    acc = _dot(yf_ref[0], w_ref[0, :k1, :])
    acc += _dot(ym_ref[0], w_ref[0, k1:k2, :])
    acc += _dot(yr_ref[0], w_ref[0, k2:, :])
    o_ref[0] = x_ref[0] + g_ref[0] * acc


def _cast_kernel(w_ref, o_ref):
    o_ref[...] = w_ref[...].astype(o_ref.dtype)


def _cast_bf16(w, name):
    depth, r, c = w.shape
    tr = 256
    return pl.pallas_call(
        _cast_kernel,
        grid=(depth, r // tr),
        in_specs=[pl.BlockSpec((1, tr, c), lambda l, i: (l, i, 0))],
        out_specs=pl.BlockSpec((1, tr, c), lambda l, i: (l, i, 0)),
        out_shape=jax.ShapeDtypeStruct(w.shape, BF16),
        compiler_params=_params("parallel", "parallel"),
        name=name,
    )(w)


def _matmul_plain(a, w_t, layer, out_dtype, tm, tn):
    batch, t, kdim = a.shape
    n = w_t.shape[1]
    tm = min(tm, t)
    return pl.pallas_call(
        _mm_plain_kernel,
        grid=(batch, t // tm, n // tn),
        in_specs=[
            pl.BlockSpec((1, tm, kdim), lambda b, i, j: (b, i, 0)),
            pl.BlockSpec((1, tn, kdim), lambda b, i, j: (layer, j, 0)),
        ],
        out_specs=pl.BlockSpec((1, tm, tn), lambda b, i, j: (b, i, j)),
        out_shape=jax.ShapeDtypeStruct((batch, t, n), out_dtype),
        compiler_params=_params("parallel", "parallel", "arbitrary"),
        name="input_projection",
    )(a, w_t)


def _matmul_swiglu(a, wg, wu, layer, tm, tn):
    batch, t, kdim = a.shape
    n = wg.shape[2]
    tm = min(tm, t)
    w_spec = pl.BlockSpec((1, kdim, tn), lambda b, i, j: (layer, 0, j))
    return pl.pallas_call(
        _mm_swiglu_kernel,
        grid=(batch, t // tm, pl.cdiv(n, tn)),
        in_specs=[pl.BlockSpec((1, tm, kdim), lambda b, i, j: (b, i, 0)), w_spec, w_spec],
        out_specs=pl.BlockSpec((1, tm, tn), lambda b, i, j: (b, i, j)),
        out_shape=jax.ShapeDtypeStruct((batch, t, n), BF16),
        compiler_params=_params("parallel", "parallel", "arbitrary"),
        name="ffn_gate_up_swiglu",
    )(a, wg, wu)


def _matmul_residual(a, w, layer, x, gate, tm, tn, tk, name):
    batch, t, kdim = a.shape
    n = w.shape[2]
    tm = min(tm, t)
    assert kdim % tk == 0 and tk % LANE == 0
    nk = kdim // tk
    return pl.pallas_call(
        functools.partial(_mm_residual_kernel, nk=nk),
        grid=(batch, t // tm, n // tn, nk),
        in_specs=[
            pl.BlockSpec((1, tm, tk), lambda b, i, j, k: (b, i, k)),
            pl.BlockSpec((1, tk, tn), lambda b, i, j, k: (layer, k, j)),
            pl.BlockSpec((1, tm, tn), lambda b, i, j, k: (b, i, j)),
            pl.BlockSpec((1, 1, tn), lambda b, i, j, k: (b, 0, j)),
        ],
        out_specs=pl.BlockSpec((1, tm, tn), lambda b, i, j, k: (b, i, j)),
        out_shape=jax.ShapeDtypeStruct((batch, t, n), F32),
        scratch_shapes=[pltpu.VMEM((tm, tn), F32)],
        compiler_params=_params("parallel", "parallel", "parallel", "arbitrary"),
        name=name,
    )(a, w, x, gate)


def _matmul_mix(y_fox, y_ml, y_rw, w, layer, x, gate, tm, tn):
    batch, t, _ = x.shape
    kdim, n = w.shape[1], w.shape[2]
    tm = min(tm, t)
    act = lambda width: pl.BlockSpec((1, tm, width), lambda b, i, j: (b, i, 0))
    return pl.pallas_call(
        _mm_mix_kernel,
        grid=(batch, t // tm, n // tn),
        in_specs=[
            act(FOX_WIDTH), act(ML_WIDTH), act(RW_WIDTH),
            pl.BlockSpec((1, kdim, tn), lambda b, i, j: (layer, 0, j)),
            pl.BlockSpec((1, tm, tn), lambda b, i, j: (b, i, j)),
            pl.BlockSpec((1, 1, tn), lambda b, i, j: (b, 0, j)),
        ],
        out_specs=pl.BlockSpec((1, tm, tn), lambda b, i, j: (b, i, j)),
        out_shape=jax.ShapeDtypeStruct((batch, t, n), F32),
        compiler_params=_params("parallel", "parallel", "arbitrary"),
        name="output_projection",
    )(y_fox, y_ml, y_rw, w, x, gate)


ML_CHUNK = 128
_CUM_ROWS = 256


def _gates_kernel(ff_ref, mif_ref, fb_ref, mb_ref, cum_ref, li_ref, bcs_ref, *, t):
    r = _CUM_ROWS if t % _CUM_ROWS == 0 else t
    row = lax.broadcasted_iota(jnp.int32, (r, r), 0)
    col = lax.broadcasted_iota(jnp.int32, (r, r), 1)
    tril = (col <= row).astype(F32)
    fb = fb_ref[...]

    def cum_body(i, carry):
        start = pl.multiple_of(i * r, r)
        lg = _log_sigmoid(ff_ref[0, pl.ds(start, r), :] + fb)
        cs = _dot(tril, lg, HIGHEST) + carry
        cum_ref[0, pl.ds(start, r), :] = cs
        return cs[r - 1:r, :]

    lax.fori_loop(0, t // r, cum_body, jnp.zeros((1, LANE), F32))

    lc = ML_CHUNK
    rowc = lax.broadcasted_iota(jnp.int32, (lc, lc), 0)
    colc = lax.broadcasted_iota(jnp.int32, (lc, lc), 1)
    trilc = (colc <= rowc).astype(F32)
    mb = mb_ref[...]

    def ml_body(i, carry):
        start = pl.multiple_of(i * lc, lc)
        z = mif_ref[0, pl.ds(start, lc), :] + mb
        capped = GATE_SOFTCAP * jnp.tanh(z / GATE_SOFTCAP)
        li_ref[0, pl.ds(start, lc), :] = capped
        bcs_ref[0, pl.ds(start, lc), :] = _dot(trilc, _log_sigmoid(capped), HIGHEST)
        return carry

    lax.fori_loop(0, t // lc, ml_body, 0)


def _gates(p, fox_bias_row, ml_bias_row):
    batch, t, _ = p.shape
    blk = lambda col: pl.BlockSpec((1, t, LANE), lambda b: (b, 0, col // LANE))
    row = pl.BlockSpec((1, LANE), lambda b: (0, 0))
    out = pl.BlockSpec((1, t, LANE), lambda b: (b, 0, 0))
    shp = jax.ShapeDtypeStruct((batch, t, LANE), F32)
    return pl.pallas_call(
        functools.partial(_gates_kernel, t=t),
        grid=(batch,),
        in_specs=[blk(P_FOX_F), blk(P_ML_IF), row, row],
        out_specs=[out, out, out],
        out_shape=[shp, shp, shp],
        compiler_params=_params("parallel"),
        name="gate_cumsums",
    )(p, p, fox_bias_row, ml_bias_row)


FOX_TQ = 512
FOX_TK = 512


def _fox_kernel(q_ref, k_ref, v_ref, fk_ref, g_ref, o_ref, kb_ref, vt_ref, *, t, tq, tk):
    i = pl.program_id(2)
    dh = FOX_HEAD_DIM

    @pl.when(i == 0)
    def _():
        for c in range(t // tk):
            rows = slice(c * tk, (c + 1) * tk)
            kb_ref[rows, :] = k_ref[0, rows, :].astype(BF16)
            vt_ref[:, rows] = jnp.transpose(v_ref[0, rows, :]).astype(BF16)

    q = (q_ref[0] * (dh ** -0.5 * LOG2_E)).astype(BF16)
    q_pos = i * tq + lax.broadcasted_iota(jnp.int32, (1, tq), 1)

    def step(j, carry, masked):
        m, l, acc = carry
        start = pl.multiple_of(j * tk, tk)
        fk = fk_ref[0, 0, pl.ds(start, tk), :] * LOG2_E
        s = _dot_nt(kb_ref[pl.ds(start, tk), :], q) - fk
        if masked:
            k_pos = start + lax.broadcasted_iota(jnp.int32, (tk, 1), 0)
            s = jnp.where(k_pos <= q_pos, s, -jnp.inf)
        m_new = jnp.maximum(m, jnp.max(s, axis=0, keepdims=True))
        p = jnp.exp2(s - m_new)
        alpha = jnp.exp2(m - m_new)
        l = alpha * l + jnp.sum(p, axis=0, keepdims=True)
        acc = alpha * acc + _dot(vt_ref[:, pl.ds(start, tk)], p.astype(BF16))
        return m_new, l, acc

    per_tile = tq // tk
    carry = (jnp.full((1, tq), -1e30, F32), jnp.zeros((1, tq), F32), jnp.zeros((dh, tq), F32))
    carry = lax.fori_loop(0, i * per_tile, functools.partial(step, masked=False), carry)
    _, l, acc = lax.fori_loop(i * per_tile, (i + 1) * per_tile, functools.partial(step, masked=True), carry)
    o = acc / l
    o = o * lax.rsqrt(jnp.mean(o * o, axis=0, keepdims=True) + NORM_EPS) * g_ref[...]
    o_ref[0] = jnp.transpose(o).astype(o_ref.dtype)


def _fox_attention(p, cum_col, out_g_col):
    batch, t, _ = p.shape
    tq = min(FOX_TQ, t)
    tk = min(FOX_TK, tq)
    dh = FOX_HEAD_DIM
    return pl.pallas_call(
        functools.partial(_fox_kernel, t=t, tq=tq, tk=tk),
        grid=(batch, FOX_HEADS, t // tq),
        in_specs=[
            pl.BlockSpec((1, tq, dh), lambda b, h, i: (b, i, P_FOX_Q // dh + h)),
            pl.BlockSpec((1, t, dh), lambda b, h, i: (b, 0, P_FOX_K // dh + h)),
            pl.BlockSpec((1, t, dh), lambda b, h, i: (b, 0, P_FOX_V // dh + h)),
            pl.BlockSpec((1, 1, t, 1), lambda b, h, i: (b, h, 0, 0)),
            pl.BlockSpec((dh, 1), lambda b, h, i: (h, 0)),
        ],
        out_specs=pl.BlockSpec((1, tq, dh), lambda b, h, i: (b, i, h)),
        out_shape=jax.ShapeDtypeStruct((batch, t, FOX_WIDTH), BF16),
        scratch_shapes=[pltpu.VMEM((t, dh), BF16), pltpu.VMEM((dh, t), BF16)],
        compiler_params=_params("parallel", "parallel", "arbitrary"),
        name="fox_attention",
    )(p, p, p, cum_col, out_g_col)


def _mlstm_kernel(q_ref, k_ref, v_ref, o_ref, grow_ref, gcol_ref, cwq_ref, cwk_ref, cbq_ref, cbk_ref,
                  ng_ref, y_ref, *, t):
    lc = ML_CHUNK
    dk, dv = ML_QK_DIM, ML_V_DIM
    row = lax.broadcasted_iota(jnp.int32, (lc, lc), 0)
    col = lax.broadcasted_iota(jnp.int32, (lc, lc), 1)
    causal = col <= row
    cwq, cwk = cwq_ref[...], cwk_ref[...]
    cbq, cbk = cbq_ref[...], cbk_ref[...]
    norm_g = ng_ref[...]

    def conv_silu(ref, start, prev_ok, w, b):
        cur = ref[0, pl.ds(start, lc), :]
        prev = ref[0, pl.ds(jnp.maximum(start - SUBLANE, 0), SUBLANE), :]
        prev = jnp.where(prev_ok, prev, 0.0)
        ext = jnp.concatenate([prev, cur], axis=0)
        out = b
        for j in range(ML_CONV):
            off = SUBLANE - (ML_CONV - 1) + j
            out = out + ext[off:off + lc, :] * w[j:j + 1, :]
        return _silu(out)

    def body(c, carry):
        cmat, nvec, m = carry
        start = pl.multiple_of(c * lc, lc)
        prev_ok = c > 0
        qt = conv_silu(q_ref, start, prev_ok, cwq, cbq) * (dk ** -0.5)
        kt = conv_silu(k_ref, start, prev_ok, cwk, cbk)
        vt = v_ref[0, pl.ds(start, lc), :]
        li_row = grow_ref[0, 0, 0:1, pl.ds(start, lc)]
        b_row = grow_ref[0, 0, 1:2, pl.ds(start, lc)]
        li_col = gcol_ref[0, 0, pl.ds(start, lc), 0:1]
        b_col = gcol_ref[0, 0, pl.ds(start, lc), 1:2]
        g = b_row[:, lc - 1:lc]

        a_inter = b_col + m
        dm = jnp.where(causal, b_col - b_row + li_row, -jnp.inf)
        m_t = jnp.maximum(a_inter, jnp.max(dm, axis=-1, keepdims=True))
        w_inter = jnp.exp(a_inter - m_t)
        qb, kb = qt.astype(BF16), kt.astype(BF16)
        s = _dot_nt(qb, kb) * jnp.exp(dm - m_t)
        num = w_inter * _dot(qb, cmat.astype(BF16)) + _dot(s.astype(BF16), vt.astype(BF16))
        den = w_inter * jnp.sum(qt * nvec, axis=-1, keepdims=True) + jnp.sum(s, axis=-1, keepdims=True)
        h = num / jnp.maximum(jnp.abs(den), jnp.exp(-m_t))

        upd = g - b_col + li_col
        m_new = jnp.maximum(g + m, jnp.max(upd, axis=0, keepdims=True))
        decay = jnp.exp(g + m - m_new)
        wk = jnp.exp(upd - m_new) * kt
        cmat = decay * cmat + _dot_tn(wk.astype(BF16), vt.astype(BF16))
        nvec = decay * nvec + jnp.sum(wk, axis=0, keepdims=True)

        hn = h * lax.rsqrt(jnp.mean(h * h, axis=-1, keepdims=True) + NORM_EPS) * norm_g
        og = o_ref[0, pl.ds(start, lc), :]
        y_ref[0, pl.ds(start, lc), :] = (hn * _sigmoid(og)).astype(y_ref.dtype)
        return cmat, nvec, m_new

    init = (jnp.zeros((dk, dv), F32), jnp.zeros((1, dk), F32), jnp.zeros((1, 1), F32))
    lax.fori_loop(0, t // lc, body, init)


def _mlstm(p, gate_rows, gate_cols, conv_w, conv_b, norm_g):
    batch, t, _ = p.shape
    dk, dv = ML_QK_DIM, ML_V_DIM
    nh = ML_HEADS
    return pl.pallas_call(
        functools.partial(_mlstm_kernel, t=t),
        grid=(batch, nh),
        in_specs=[
            pl.BlockSpec((1, t, dk), lambda b, h: (b, 0, P_ML_Q // dk + h)),
            pl.BlockSpec((1, t, dk), lambda b, h: (b, 0, P_ML_K // dk + h)),
            pl.BlockSpec((1, t, dv), lambda b, h: (b, 0, P_ML_V // dv + h)),
            pl.BlockSpec((1, t, dv), lambda b, h: (b, 0, P_ML_O // dv + h)),
            pl.BlockSpec((1, 1, 2, t), lambda b, h: (b, h, 0, 0)),
            pl.BlockSpec((1, 1, t, 2), lambda b, h: (b, h, 0, 0)),
            pl.BlockSpec((ML_CONV, dk), lambda b, h: (0, h)),
            pl.BlockSpec((ML_CONV, dk), lambda b, h: (0, nh + h)),
            pl.BlockSpec((1, dk), lambda b, h: (0, h)),
            pl.BlockSpec((1, dk), lambda b, h: (0, nh + h)),
            pl.BlockSpec((1, dv), lambda b, h: (0, h)),
        ],
        out_specs=pl.BlockSpec((1, t, dv), lambda b, h: (b, 0, h)),
        out_shape=jax.ShapeDtypeStruct((batch, t, ML_WIDTH), BF16),
        compiler_params=_params("parallel", "parallel"),
        name="mlstm_chunkwise",
    )(p, p, p, p, gate_rows, gate_cols, conv_w, conv_w, conv_b, conv_b, norm_g)


RW_CHUNK = 64
RW_TILE = 512


def _rw_kernel(r_ref, k_ref, v_ref, wl_ref, al_ref, gl_ref,
               rp_ref, kp_ref, vp_ref, wlp_ref, alp_ref, glp_ref,
               mur_ref, muk_ref, muv_ref, muw_ref, mua_ref, mug_ref,
               w0_ref, a0_ref, kkw_ref, kaw_ref, rkw_ref, lnw_ref, lnb_ref,
               wup_ref, aup_ref, gup_ref, y_ref, s_ref, *, tm):
    i = pl.program_id(2)
    lc = RW_CHUNK
    n = RW_HEAD_DIM
    pair = LANE

    @pl.when(i == 0)
    def _():
        s_ref[...] = jnp.zeros_like(s_ref)

    first = lax.broadcasted_iota(jnp.int32, (tm, 1), 0) == 0
    has_prev = i > 0

    def shift(cur_ref, prev_ref, mu_ref):
        cur = cur_ref[0]
        last = jnp.where(has_prev, prev_ref[0][SUBLANE - 1:SUBLANE, :], 0.0)
        prev = jnp.where(first, last, pltpu.roll(cur, 1, axis=0))
        return cur + (prev - cur) * mu_ref[...]

    r = shift(r_ref, rp_ref, mur_ref)
    k = shift(k_ref, kp_ref, muk_ref)
    v = shift(v_ref, vp_ref, muv_ref)
    wl = shift(wl_ref, wlp_ref, muw_ref)
    al = shift(al_ref, alp_ref, mua_ref)
    gl = shift(gl_ref, glp_ref, mug_ref)

    hrow = lax.broadcasted_iota(jnp.int32, (pair, pair), 0)
    hcol = lax.broadcasted_iota(jnp.int32, (pair, pair), 1)
    same_head = jnp.where(hrow // n == hcol // n, 1.0, 0.0).astype(BF16)

    def head_sum(x):
        hi = x.astype(BF16)
        lo = (x - hi.astype(F32)).astype(BF16)
        return _dot(hi, same_head) + _dot(lo, same_head)

    z = w0_ref[...] + _dot(jnp.tanh(wl).astype(BF16), wup_ref[...])
    lw = -jnp.exp(_log_sigmoid(z) - 0.5)
    a = _sigmoid(a0_ref[...] + _dot(al.astype(BF16), aup_ref[...]))
    gate = _dot(_sigmoid(gl).astype(BF16), gup_ref[...])
    kk = k * kkw_ref[...]
    kk = kk / jnp.maximum(jnp.sqrt(head_sum(kk * kk)), 1e-12)
    k = k * (1.0 + (a - 1.0) * kaw_ref[...])
    bb = kk * a
    bonus = head_sum(r * k * rkw_ref[...]) * v

    trow = lax.broadcasted_iota(jnp.int32, (lc, lc), 0)
    tcol = lax.broadcasted_iota(jnp.int32, (lc, lc), 1)
    tri = jnp.where(tcol <= trow, 1.0, 0.0).astype(BF16)
    lw_hi = lw.astype(BF16)
    rem = lw - lw_hi.astype(F32)
    lw_mid = rem.astype(BF16)
    lw_lo = (rem - lw_mid.astype(F32)).astype(BF16)
    nchunk = tm // lc
    cums, ends = [], []
    for c in range(nchunk):
        sl = slice(c * lc, (c + 1) * lc)
        cum_c = _dot(tri, lw_hi[sl]) + (_dot(tri, lw_mid[sl]) + _dot(tri, lw_lo[sl]))
        cums.append(cum_c)
        ends.append(jnp.broadcast_to(cum_c[lc - 1:lc, :], (lc, LANE)))
    cum = jnp.concatenate(cums, axis=0)
    cum_end = jnp.concatenate(ends, axis=0)
    e_inc = jnp.exp(cum)
    e_neg = jnp.exp(-cum)
    tail = jnp.exp(cum_end - cum)
    rt = r * e_inc
    at = kk * jnp.exp(cum - lw)
    kt = k * e_neg
    bt = bb * e_neg
    kh = k * tail
    bh = bb * tail

    head0 = lax.broadcasted_iota(jnp.int32, (1, LANE), 1) < n

    def stack(x, dtype=BF16):
        return jnp.concatenate([jnp.where(head0, x, 0.0), jnp.where(head0, 0.0, x)], axis=0).astype(dtype)

    prow = lax.broadcasted_iota(jnp.int32, (pair, pair), 0)
    pcol = lax.broadcasted_iota(jnp.int32, (pair, pair), 1)
    lower_strict = pcol < prow
    lower_incl = pcol <= prow
    eye = jnp.where(pcol == prow, 1.0, 0.0)

    chunks = range(nchunk)
    rows = [slice(c * lc, (c + 1) * lc) for c in chunks]
    a2 = [stack(at[sl]) for sl in rows]
    r2 = [stack(rt[sl], F32) for sl in rows]
    v2 = [stack(v[sl]) for sl in rows]
    mm = [_dot_nt(jnp.concatenate([a2[c], r2[c].astype(BF16)], axis=0),
                  jnp.concatenate([stack(bt[rows[c]]), stack(kt[rows[c]])], axis=0)) for c in chunks]
    m_ak = [jnp.where(lower_strict, mm[c][:pair, pair:], 0.0).astype(BF16) for c in chunks]
    m_rb = [jnp.where(lower_incl, mm[c][pair:, :pair], 0.0).astype(BF16) for c in chunks]
    m_rk = [jnp.where(lower_incl, mm[c][pair:, pair:], 0.0).astype(BF16) for c in chunks]

    m_ab = [jnp.where(lower_strict, mm[c][:pair, :pair], 0.0) for c in chunks]
    i_plus_m = [(eye + m_ab[c]).astype(BF16) for c in chunks]
    t_inv = [eye - m_ab[c] for c in chunks]
    for _ in range((lc - 1).bit_length() - 1):
        res = [_dot(i_plus_m[c], t_inv[c].astype(BF16)) for c in chunks]
        t_inv = [_dot(t_inv[c].astype(BF16), (2.0 * eye - res[c]).astype(BF16)) for c in chunks]

    x1 = [_dot(m_ak[c], v2[c]) for c in chunks]
    aub = [_dot(t_inv[c].astype(BF16), jnp.concatenate([a2[c], x1[c].astype(BF16)], axis=1)).astype(BF16)
           for c in chunks]
    rb = [_dot(m_rb[c], aub[c]) for c in chunks]
    r_eff = [r2[c] - rb[c][:, :pair] for c in chunks]
    y_loc = [_dot(m_rk[c], v2[c]) - rb[c][:, pair:] for c in chunks]
    r_eff = [(r_eff[c][:lc] + r_eff[c][lc:]).astype(BF16) for c in chunks]
    y_loc = [y_loc[c][:lc] + y_loc[c][lc:] for c in chunks]
    gd = [_dot_tn(aub[c], stack(bh[rows[c]])) for c in chunks]
    g_mat = [(eye * e_inc[(c + 1) * lc - 1:(c + 1) * lc, :] - gd[c][:pair]).astype(BF16) for c in chunks]
    d_mat = [_dot_tn(v2[c], stack(kh[rows[c]])) - gd[c][pair:] for c in chunks]

    s_mat = s_ref[...]
    ys = []
    for c in chunks:
        s_b = s_mat.astype(BF16)
        ys.append(_dot_nt(r_eff[c], s_b) + y_loc[c])
        s_mat = _dot(s_b, g_mat[c]) + d_mat[c]
    s_ref[...] = s_mat

    y = jnp.concatenate(ys, axis=0)
    mean = head_sum(y) * (1.0 / n)
    cen = y - mean
    var = head_sum(cen * cen) * (1.0 / n)
    y = cen * lax.rsqrt(var + RW_LN_EPS) * lnw_ref[...] + lnb_ref[...]
    y_ref[0] = ((y + bonus) * gate).astype(y_ref.dtype)


def _rwkv7(p, mu_pad, w0, a0, k_k, k_a, r_k, ln_w, ln_b, w_up, a_up, g_up_pad):
    batch, t, _ = p.shape
    tm = min(RW_TILE, t)
    nj = RW_WIDTH // LANE
    gw = RW_GATE_LORA_PAD
    rb = tm // SUBLANE

    def cur(col, width=LANE, tiled=True):
        base = col // width
        if tiled:
            return pl.BlockSpec((1, tm, width), lambda b, j, i: (b, i, base + j))
        return pl.BlockSpec((1, tm, width), lambda b, j, i: (b, i, base))

    def prev(col, width=LANE, tiled=True):
        base = col // width
        if tiled:
            return pl.BlockSpec((1, SUBLANE, width), lambda b, j, i: (b, jnp.maximum(i * rb - 1, 0), base + j))
        return pl.BlockSpec((1, SUBLANE, width), lambda b, j, i: (b, jnp.maximum(i * rb - 1, 0), base))

    def murow(col, width=LANE, tiled=True):
        base = col // width
        if tiled:
            return pl.BlockSpec((1, width), lambda b, j, i: (0, base + j))
        return pl.BlockSpec((1, width), lambda b, j, i: (0, base))

    vec = pl.BlockSpec((1, LANE), lambda b, j, i: (0, j))
    up = pl.BlockSpec((RW_LORA, LANE), lambda b, j, i: (0, j))
    gup = pl.BlockSpec((gw, LANE), lambda b, j, i: (0, j))
    return pl.pallas_call(
        functools.partial(_rw_kernel, tm=tm),
        grid=(batch, nj, t // tm),
        in_specs=[
            cur(P_RW_R), cur(P_RW_K), cur(P_RW_V),
            cur(P_RW_WL, tiled=False), cur(P_RW_AL, tiled=False), cur(P_RW_GL, gw, tiled=False),
            prev(P_RW_R), prev(P_RW_K), prev(P_RW_V),
            prev(P_RW_WL, tiled=False), prev(P_RW_AL, tiled=False), prev(P_RW_GL, gw, tiled=False),
            murow(P_RW_R), murow(P_RW_K), murow(P_RW_V),
            murow(P_RW_WL, tiled=False), murow(P_RW_AL, tiled=False), murow(P_RW_GL, gw, tiled=False),
            vec, vec, vec, vec, vec, vec, vec, up, up, gup,
        ],
        out_specs=pl.BlockSpec((1, tm, LANE), lambda b, j, i: (b, i, j)),
        out_shape=jax.ShapeDtypeStruct((batch, t, RW_WIDTH), BF16),
        scratch_shapes=[pltpu.VMEM((LANE, LANE), F32)],
        compiler_params=_params("parallel", "parallel", "arbitrary"),
        name="rwkv7_chunkwise",
    )(p, p, p, p, p, p, p, p, p, p, p, p,
      mu_pad, mu_pad, mu_pad, mu_pad, mu_pad, mu_pad,
      w0, a0, k_k, k_a, r_k, ln_w, ln_b, w_up, a_up, g_up_pad)


def _row(v, width=None):
    v = v.reshape(1, -1).astype(F32)
    if width is not None and v.shape[1] < width:
        v = jnp.pad(v, ((0, 0), (0, width - v.shape[1])))
    return v


def kernel(x, c, ada_w, ada_b, norm1, w_in, fox_f_bias, fox_norm, ml_conv_w, ml_conv_b, ml_i_bias, ml_f_bias,
           ml_norm, rw_mu, rw_w0, rw_w_up, rw_a0, rw_a_up, rw_g_up, rw_k_k, rw_k_a, rw_r_k, rw_ln_w, rw_ln_b,
           w_out, norm2, ffn_gate, ffn_up, ffn_down, final_norm):
    batch, t, d = x.shape
    depth = ada_w.shape[0]
    assert d == D_MODEL and t % ML_CHUNK == 0 and t % RW_CHUNK == 0

    mod = _modulation(c, ada_w, ada_b)
    mod = mod.reshape(depth, batch, 6, 1, d)
    w_in_t = _repack_w_in(w_in)
    w_out_b = _cast_bf16(w_out, "cast_w_out")
    w_gate_b = _cast_bf16(ffn_gate, "cast_ffn_gate")
    w_up_b = _cast_bf16(ffn_up, "cast_ffn_up")
    w_down_b = _cast_bf16(ffn_down, "cast_ffn_down")

    for l in range(depth):
        sh1, sc1, g1, sh2, sc2, g2 = (mod[l, :, i] for i in range(6))

        h = _norm_mod(x, norm1[l], sc1, sh1, BF16)
        p = _matmul_plain(h, w_in_t, l, F32, tm=1024, tn=512)

        fox_bias_row = _row(fox_f_bias[l], LANE)
        ml_bias_row = _row(jnp.concatenate([ml_i_bias[l], ml_f_bias[l]]), LANE)
        cum, li_all, bcs_all = _gates(p, fox_bias_row, ml_bias_row)
        cum_h = cum[:, :, :FOX_HEADS].transpose(0, 2, 1)
        y_fox = _fox_attention(p, cum_h[..., None], fox_norm[l].reshape(FOX_WIDTH, 1))

        li = li_all[:, :, :ML_HEADS]
        bcs = bcs_all[:, :, ML_HEADS:2 * ML_HEADS]
        gate_cols = jnp.stack([li, bcs], axis=-1).transpose(0, 2, 1, 3)
        gate_rows = gate_cols.transpose(0, 1, 3, 2)
        y_ml = _mlstm(p, gate_rows, gate_cols, ml_conv_w[l], _row(ml_conv_b[l]), _row(ml_norm[l]))

        mu_full = jnp.zeros((1, N_IN), F32).at[:, _O_RW:].set(rw_mu[l][None, :])
        mu_pad = _pad_columns(mu_full)
        g_up_pad = jnp.pad(rw_g_up[l], ((RW_GATE_LORA_PAD - RW_GATE_LORA, 0), (0, 0))).astype(BF16)
        y_rw = _rwkv7(p, mu_pad, _row(rw_w0[l]), _row(rw_a0[l]), _row(rw_k_k[l]), _row(rw_k_a[l]),
                      _row(rw_r_k[l]), _row(rw_ln_w[l]), _row(rw_ln_b[l]),
                      rw_w_up[l].astype(BF16), rw_a_up[l].astype(BF16), g_up_pad)

        x = _matmul_mix(y_fox, y_ml, y_rw, w_out_b, l, x, g1, tm=1024, tn=512)

        h = _norm_mod(x, norm2[l], sc2, sh2, BF16)
        act = _matmul_swiglu(h, w_gate_b, w_up_b, l, tm=1024, tn=512)
        x = _matmul_residual(act, w_down_b, l, x, g2, tm=1024, tn=512, tk=D_FF // 2, name="ffn_down")

    zeros = jnp.zeros((batch, 1, d), F32)
    return _norm_mod(x, final_norm, zeros, zeros, F32)
```

```python
import functools

import jax
import jax.numpy as jnp
from jax import lax
from jax.experimental import pallas as pl
from jax.experimental.pallas import tpu as pltpu

F32 = jnp.float32
BF16 = jnp.bfloat16
HIGHEST = lax.Precision.HIGHEST

D_MODEL = 4096
FOX_HEADS, FOX_HEAD_DIM = 12, 128
FOX_WIDTH = FOX_HEADS * FOX_HEAD_DIM
ML_HEADS, ML_QK_DIM, ML_V_DIM = 4, 128, 256
ML_QK_WIDTH = ML_HEADS * ML_QK_DIM
ML_WIDTH = ML_HEADS * ML_V_DIM
ML_CONV = 4
GATE_SOFTCAP = 15.0
RW_HEADS, RW_HEAD_DIM = 24, 64
RW_WIDTH = RW_HEADS * RW_HEAD_DIM
RW_LORA = 128
RW_GATE_LORA = 480
RW_GATE_LORA_PAD = 512
RW_LN_EPS = 64e-5
D_FF = 11008
NORM_EPS = 1e-6
LOG2_E = 1.4426950408889634

LANE = 128
SUBLANE = 8
VMEM_LIMIT = 56 * 1024 * 1024

_O_FOX_Q, _O_FOX_K, _O_FOX_V, _O_FOX_F = 0, 1536, 3072, 4608
_O_ML = 4620
_O_ML_QK, _O_ML_V, _O_ML_I, _O_ML_F, _O_ML_O = _O_ML, _O_ML + 1024, _O_ML + 2048, _O_ML + 2052, _O_ML + 2056
_O_RW = 7700
_O_RW_R, _O_RW_K, _O_RW_V = _O_RW, _O_RW + 1536, _O_RW + 3072
_O_RW_WL, _O_RW_AL, _O_RW_GL = _O_RW + 4608, _O_RW + 4736, _O_RW + 4864
N_IN = 13044
P_FOX_Q, P_FOX_K, P_FOX_V = 0, 1536, 3072
P_ML_V, P_ML_O, P_ML_Q, P_ML_K = 4608, 5632, 6656, 7168
P_RW_R, P_RW_K, P_RW_V = 7680, 9216, 10752
P_RW_GL, P_RW_WL, P_RW_AL = 12288, 12800, 12928
P_FOX_F, P_ML_IF = 13056, 13184
N_PAD = 13312
_SECTIONS = (
    (P_FOX_Q, _O_FOX_Q, 3 * FOX_WIDTH),
    (P_ML_V, _O_ML_V, ML_WIDTH),
    (P_ML_O, _O_ML_O, ML_WIDTH),
    (P_ML_Q, _O_ML_QK, 2 * ML_QK_WIDTH),
    (P_RW_R, _O_RW_R, 3 * RW_WIDTH),
    (P_RW_GL + RW_GATE_LORA_PAD - RW_GATE_LORA, _O_RW_GL, RW_GATE_LORA),
    (P_RW_WL, _O_RW_WL, 2 * RW_LORA),
    (P_FOX_F, _O_FOX_F, FOX_HEADS),
    (P_ML_IF, _O_ML_I, 2 * ML_HEADS),
)


def _pad_columns(w):
    parts, pos = [], 0
    for p_start, o_start, width in _SECTIONS:
        if p_start > pos:
            parts.append(jnp.zeros(w.shape[:-1] + (p_start - pos,), w.dtype))
        parts.append(w[..., o_start:o_start + width])
        pos = p_start + width
    if pos < N_PAD:
        parts.append(jnp.zeros(w.shape[:-1] + (N_PAD - pos,), w.dtype))
    return jnp.concatenate(parts, axis=-1)


def _params(*sem):
    return pltpu.CompilerParams(dimension_semantics=sem, vmem_limit_bytes=VMEM_LIMIT)


def _repack_table():
    src, lo, hi = [0] * (N_PAD // LANE), [0] * (N_PAD // LANE), [0] * (N_PAD // LANE)
    for p_start, o_start, width in _SECTIONS:
        first = p_start // LANE
        last = (p_start + width - 1) // LANE
        for tile in range(first, last + 1):
            base = tile * LANE
            src[tile] = o_start + (base - p_start)
            lo[tile] = max(p_start - base, 0)
            hi[tile] = min(p_start + width - base, LANE)
            assert 0 <= src[tile] and src[tile] + LANE <= N_IN
    return (jnp.asarray(src, jnp.int32), jnp.asarray(lo, jnp.int32), jnp.asarray(hi, jnp.int32))


def _repack_kernel(src_ref, lo_ref, hi_ref, w_ref, o_ref):
    i = pl.program_id(0)
    row = lax.broadcasted_iota(jnp.int32, (LANE, 1), 0)
    valid = (row >= lo_ref[i]) & (row < hi_ref[i])
    for layer in range(o_ref.shape[0]):
        o_ref[layer] = jnp.where(valid, w_ref[:, layer, :], 0.0).astype(o_ref.dtype)


def _repack_w_in(w_in):
    depth, kdim, _ = w_in.shape
    w_t = jnp.transpose(w_in, (2, 0, 1))
    src, lo, hi = _repack_table()
    return pl.pallas_call(
        _repack_kernel,
        grid_spec=pltpu.PrefetchScalarGridSpec(
            num_scalar_prefetch=3,
            grid=(N_PAD // LANE,),
            in_specs=[pl.BlockSpec((pl.Element(LANE), pl.Element(depth), pl.Element(kdim)),
                                   lambda i, src, lo, hi: (src[i], 0, 0))],
            out_specs=pl.BlockSpec((depth, LANE, kdim), lambda i, src, lo, hi: (0, i, 0)),
        ),
        out_shape=jax.ShapeDtypeStruct((depth, N_PAD, kdim), BF16),
        compiler_params=_params("parallel"),
        name="repack_w_in",
    )(src, lo, hi, w_t)


def _log_sigmoid(x):
    return jnp.minimum(x, 0.0) - jnp.log1p(jnp.exp(-jnp.abs(x)))


def _sigmoid(x):
    return 1.0 / (1.0 + jnp.exp(-x))


def _silu(x):
    return x * _sigmoid(x)


def _dot(a, b, precision=None):
    return jnp.dot(a, b, preferred_element_type=F32, precision=precision)


def _dot_nt(a, b, precision=None):
    return lax.dot_general(a, b, (((1,), (1,)), ((), ())), preferred_element_type=F32, precision=precision)


def _dot_tn(a, b, precision=None):
    return lax.dot_general(a, b, (((0,), (0,)), ((), ())), preferred_element_type=F32, precision=precision)


def _mod_kernel(c_ref, w_ref, b_ref, o_ref):
    a = _silu(c_ref[...]).astype(BF16)
    o_ref[0] = _dot(a, w_ref[0].astype(BF16)) + b_ref[0]


def _modulation(c, ada_w, ada_b):
    depth, d, n = ada_w.shape
    batch = c.shape[0]
    rows = -(-batch // SUBLANE) * SUBLANE
    c_pad = jnp.zeros((rows, d), F32).at[:batch].set(c)
    tn = 512
    out = pl.pallas_call(
        _mod_kernel,
        grid=(depth, n // tn),
        in_specs=[
            pl.BlockSpec((rows, d), lambda l, j: (0, 0)),
            pl.BlockSpec((1, d, tn), lambda l, j: (l, 0, j)),
            pl.BlockSpec((1, 1, tn), lambda l, j: (l, 0, j)),
        ],
        out_specs=pl.BlockSpec((1, rows, tn), lambda l, j: (l, 0, j)),
        out_shape=jax.ShapeDtypeStruct((depth, rows, n), F32),
        compiler_params=_params("parallel", "parallel"),
        name="adaln_modulation",
    )(c_pad, ada_w, ada_b.reshape(depth, 1, n))
    return out[:, :batch]


def _norm_kernel(x_ref, g_ref, sc_ref, sh_ref, o_ref):
    x = x_ref[0]
    y = x * lax.rsqrt(jnp.mean(x * x, axis=-1, keepdims=True) + NORM_EPS)
    y = (y * g_ref[...]) * (1.0 + sc_ref[0]) + sh_ref[0]
    o_ref[0] = y.astype(o_ref.dtype)


def _norm_mod(x, g, scale, shift, out_dtype):
    batch, t, d = x.shape
    tm = min(512, t)
    return pl.pallas_call(
        _norm_kernel,
        grid=(batch, t // tm),
        in_specs=[
            pl.BlockSpec((1, tm, d), lambda b, i: (b, i, 0)),
            pl.BlockSpec((1, d), lambda b, i: (0, 0)),
            pl.BlockSpec((1, 1, d), lambda b, i: (b, 0, 0)),
            pl.BlockSpec((1, 1, d), lambda b, i: (b, 0, 0)),
        ],
        out_specs=pl.BlockSpec((1, tm, d), lambda b, i: (b, i, 0)),
        out_shape=jax.ShapeDtypeStruct((batch, t, d), out_dtype),
        compiler_params=_params("parallel", "parallel"),
        name="rmsnorm_modulate",
    )(x, g.reshape(1, d), scale, shift)


def _mm_plain_kernel(a_ref, wt_ref, o_ref):
    o_ref[0] = _dot_nt(a_ref[0], wt_ref[0]).astype(o_ref.dtype)


def _mm_swiglu_kernel(a_ref, wg_ref, wu_ref, o_ref):
    a = a_ref[0]
    gate = _dot(a, wg_ref[0])
    up = _dot(a, wu_ref[0])
    o_ref[0] = (_silu(gate) * up).astype(o_ref.dtype)


def _mm_residual_kernel(a_ref, w_ref, x_ref, g_ref, o_ref, acc_ref, *, nk):
    k = pl.program_id(3)

    @pl.when(k == 0)
    def _():
        acc_ref[...] = jnp.zeros_like(acc_ref)

    acc_ref[...] += _dot(a_ref[0], w_ref[0])

    @pl.when(k == nk - 1)
    def _():
        o_ref[0] = x_ref[0] + g_ref[0] * acc_ref[...]


def _mm_mix_kernel(yf_ref, ym_ref, yr_ref, w_ref, x_ref, g_ref, o_ref):
    k1, k2 = FOX_WIDTH, FOX_WIDTH + ML_WIDTH
    acc = _dot(yf_ref[0], w_ref[0, :k1, :])
    acc += _dot(ym_ref[0], w_ref[0, k1:k2, :])
    acc += _dot(yr_ref[0], w_ref[0, k2:, :])
    o_ref[0] = x_ref[0] + g_ref[0] * acc


def _cast_kernel(w_ref, o_ref):
    o_ref[...] = w_ref[...].astype(o_ref.dtype)


def _cast_bf16(w, name):
    depth, r, c = w.shape
    tr = 256
    return pl.pallas_call(
        _cast_kernel,
        grid=(depth, r // tr),
        in_specs=[pl.BlockSpec((1, tr, c), lambda l, i: (l, i, 0))],
        out_specs=pl.BlockSpec((1, tr, c), lambda l, i: (l, i, 0)),
        out_shape=jax.ShapeDtypeStruct(w.shape, BF16),
        compiler_params=_params("parallel", "parallel"),
        name=name,
    )(w)


def _matmul_plain(a, w_t, layer, out_dtype, tm, tn):
    batch, t, kdim = a.shape
    n = w_t.shape[1]
    tm = min(tm, t)
    return pl.pallas_call(
        _mm_plain_kernel,
        grid=(batch, t // tm, n // tn),
        in_specs=[
            pl.BlockSpec((1, tm, kdim), lambda b, i, j: (b, i, 0)),
            pl.BlockSpec((1, tn, kdim), lambda b, i, j: (layer, j, 0)),
        ],
        out_specs=pl.BlockSpec((1, tm, tn), lambda b, i, j: (b, i, j)),
        out_shape=jax.ShapeDtypeStruct((batch, t, n), out_dtype),
        compiler_params=_params("parallel", "parallel", "arbitrary"),
        name="input_projection",
    )(a, w_t)


def _matmul_swiglu(a, wg, wu, layer, tm, tn):
    batch, t, kdim = a.shape
    n = wg.shape[2]
    tm = min(tm, t)
    w_spec = pl.BlockSpec((1, kdim, tn), lambda b, i, j: (layer, 0, j))
    return pl.pallas_call(
        _mm_swiglu_kernel,
        grid=(batch, t // tm, pl.cdiv(n, tn)),
        in_specs=[pl.BlockSpec((1, tm, kdim), lambda b, i, j: (b, i, 0)), w_spec, w_spec],
        out_specs=pl.BlockSpec((1, tm, tn), lambda b, i, j: (b, i, j)),
        out_shape=jax.ShapeDtypeStruct((batch, t, n), BF16),
        compiler_params=_params("parallel", "parallel", "arbitrary"),
        name="ffn_gate_up_swiglu",
    )(a, wg, wu)


def _matmul_residual(a, w, layer, x, gate, tm, tn, tk, name):
    batch, t, kdim = a.shape
    n = w.shape[2]
    tm = min(tm, t)
    assert kdim % tk == 0 and tk % LANE == 0
    nk = kdim // tk
    return pl.pallas_call(
        functools.partial(_mm_residual_kernel, nk=nk),
        grid=(batch, t // tm, n // tn, nk),
        in_specs=[
            pl.BlockSpec((1, tm, tk), lambda b, i, j, k: (b, i, k)),
            pl.BlockSpec((1, tk, tn), lambda b, i, j, k: (layer, k, j)),
            pl.BlockSpec((1, tm, tn), lambda b, i, j, k: (b, i, j)),
            pl.BlockSpec((1, 1, tn), lambda b, i, j, k: (b, 0, j)),
        ],
        out_specs=pl.BlockSpec((1, tm, tn), lambda b, i, j, k: (b, i, j)),
        out_shape=jax.ShapeDtypeStruct((batch, t, n), F32),
        scratch_shapes=[pltpu.VMEM((tm, tn), F32)],
        compiler_params=_params("parallel", "parallel", "parallel", "arbitrary"),
        name=name,
    )(a, w, x, gate)


def _matmul_mix(y_fox, y_ml, y_rw, w, layer, x, gate, tm, tn):
    batch, t, _ = x.shape
    kdim, n = w.shape[1], w.shape[2]
    tm = min(tm, t)
    act = lambda width: pl.BlockSpec((1, tm, width), lambda b, i, j: (b, i, 0))
    return pl.pallas_call(
        _mm_mix_kernel,
        grid=(batch, t // tm, n // tn),
        in_specs=[
            act(FOX_WIDTH), act(ML_WIDTH), act(RW_WIDTH),
            pl.BlockSpec((1, kdim, tn), lambda b, i, j: (layer, 0, j)),
            pl.BlockSpec((1, tm, tn), lambda b, i, j: (b, i, j)),
            pl.BlockSpec((1, 1, tn), lambda b, i, j: (b, 0, j)),
        ],
        out_specs=pl.BlockSpec((1, tm, tn), lambda b, i, j: (b, i, j)),
        out_shape=jax.ShapeDtypeStruct((batch, t, n), F32),
        compiler_params=_params("parallel", "parallel", "arbitrary"),
        name="output_projection",
    )(y_fox, y_ml, y_rw, w, x, gate)


ML_CHUNK = 128
_CUM_ROWS = 256


def _gates_kernel(ff_ref, mif_ref, fb_ref, mb_ref, cum_ref, li_ref, bcs_ref, *, t):
    r = _CUM_ROWS if t % _CUM_ROWS == 0 else t
    row = lax.broadcasted_iota(jnp.int32, (r, r), 0)
    col = lax.broadcasted_iota(jnp.int32, (r, r), 1)
    tril = (col <= row).astype(F32)
    fb = fb_ref[...]

    def cum_body(i, carry):
        start = pl.multiple_of(i * r, r)
        lg = _log_sigmoid(ff_ref[0, pl.ds(start, r), :] + fb)
        cs = _dot(tril, lg, HIGHEST) + carry
        cum_ref[0, pl.ds(start, r), :] = cs
        return cs[r - 1:r, :]

    lax.fori_loop(0, t // r, cum_body, jnp.zeros((1, LANE), F32))

    lc = ML_CHUNK
    rowc = lax.broadcasted_iota(jnp.int32, (lc, lc), 0)
    colc = lax.broadcasted_iota(jnp.int32, (lc, lc), 1)
    trilc = (colc <= rowc).astype(F32)
    mb = mb_ref[...]

    def ml_body(i, carry):
        start = pl.multiple_of(i * lc, lc)
        z = mif_ref[0, pl.ds(start, lc), :] + mb
        capped = GATE_SOFTCAP * jnp.tanh(z / GATE_SOFTCAP)
        li_ref[0, pl.ds(start, lc), :] = capped
        bcs_ref[0, pl.ds(start, lc), :] = _dot(trilc, _log_sigmoid(capped), HIGHEST)
        return carry

    lax.fori_loop(0, t // lc, ml_body, 0)


def _gates(p, fox_bias_row, ml_bias_row):
    batch, t, _ = p.shape
    blk = lambda col: pl.BlockSpec((1, t, LANE), lambda b: (b, 0, col // LANE))
    row = pl.BlockSpec((1, LANE), lambda b: (0, 0))
    out = pl.BlockSpec((1, t, LANE), lambda b: (b, 0, 0))
    shp = jax.ShapeDtypeStruct((batch, t, LANE), F32)
    return pl.pallas_call(
        functools.partial(_gates_kernel, t=t),
        grid=(batch,),
        in_specs=[blk(P_FOX_F), blk(P_ML_IF), row, row],
        out_specs=[out, out, out],
        out_shape=[shp, shp, shp],
        compiler_params=_params("parallel"),
        name="gate_cumsums",
    )(p, p, fox_bias_row, ml_bias_row)


FOX_TQ = 1024
FOX_TK = 512


def _fox_kernel(q_ref, k_ref, v_ref, fk_ref, g_ref, o_ref, kb_ref, vt_ref, *, t, tq, tk):
    i = pl.program_id(2)
    dh = FOX_HEAD_DIM

    @pl.when(i == 0)
    def _():
        for c in range(t // tk):
            rows = slice(c * tk, (c + 1) * tk)
            kb_ref[rows, :] = k_ref[0, rows, :].astype(BF16)
            vt_ref[:, rows] = jnp.transpose(v_ref[0, rows, :]).astype(BF16)

    q = (q_ref[0] * (dh ** -0.5 * LOG2_E)).astype(BF16)
    q_pos = i * tq + lax.broadcasted_iota(jnp.int32, (1, tq), 1)

    def step(j, carry, masked):
        m, l, acc = carry
        start = pl.multiple_of(j * tk, tk)
        fk = fk_ref[0, 0, pl.ds(start, tk), :] * LOG2_E
        s = _dot_nt(kb_ref[pl.ds(start, tk), :], q) - fk
        if masked:
            k_pos = start + lax.broadcasted_iota(jnp.int32, (tk, 1), 0)
            s = jnp.where(k_pos <= q_pos, s, -jnp.inf)
        m_new = jnp.maximum(m, jnp.max(s, axis=0, keepdims=True))
        p = jnp.exp2(s - m_new)
        alpha = jnp.exp2(m - m_new)
        l = alpha * l + jnp.sum(p, axis=0, keepdims=True)
        acc = alpha * acc + _dot(vt_ref[:, pl.ds(start, tk)], p.astype(BF16))
        return m_new, l, acc

    per_tile = tq // tk
    carry = (jnp.full((1, tq), -1e30, F32), jnp.zeros((1, tq), F32), jnp.zeros((dh, tq), F32))
    carry = lax.fori_loop(0, i * per_tile, functools.partial(step, masked=False), carry)
    _, l, acc = lax.fori_loop(i * per_tile, (i + 1) * per_tile, functools.partial(step, masked=True), carry)
    o = acc / l
    o = o * lax.rsqrt(jnp.mean(o * o, axis=0, keepdims=True) + NORM_EPS) * g_ref[...]
    o_ref[0] = jnp.transpose(o).astype(o_ref.dtype)


def _fox_attention(p, cum_col, out_g_col):
    batch, t, _ = p.shape
    tq = min(FOX_TQ, t)
    tk = min(FOX_TK, tq)
    dh = FOX_HEAD_DIM
    return pl.pallas_call(
        functools.partial(_fox_kernel, t=t, tq=tq, tk=tk),
        grid=(batch, FOX_HEADS, t // tq),
        in_specs=[
            pl.BlockSpec((1, tq, dh), lambda b, h, i: (b, i, P_FOX_Q // dh + h)),
            pl.BlockSpec((1, t, dh), lambda b, h, i: (b, 0, P_FOX_K // dh + h)),
            pl.BlockSpec((1, t, dh), lambda b, h, i: (b, 0, P_FOX_V // dh + h)),
            pl.BlockSpec((1, 1, t, 1), lambda b, h, i: (b, h, 0, 0)),
            pl.BlockSpec((dh, 1), lambda b, h, i: (h, 0)),
        ],
        out_specs=pl.BlockSpec((1, tq, dh), lambda b, h, i: (b, i, h)),
        out_shape=jax.ShapeDtypeStruct((batch, t, FOX_WIDTH), BF16),
        scratch_shapes=[pltpu.VMEM((t, dh), BF16), pltpu.VMEM((dh, t), BF16)],
        compiler_params=_params("parallel", "parallel", "arbitrary"),
        name="fox_attention",
    )(p, p, p, cum_col, out_g_col)


def _mlstm_kernel(q_ref, k_ref, v_ref, o_ref, grow_ref, gcol_ref, cwq_ref, cwk_ref, cbq_ref, cbk_ref,
                  ng_ref, y_ref, *, t):
    lc = ML_CHUNK
    dk, dv = ML_QK_DIM, ML_V_DIM
    row = lax.broadcasted_iota(jnp.int32, (lc, lc), 0)
    col = lax.broadcasted_iota(jnp.int32, (lc, lc), 1)
    causal = col <= row
    cwq, cwk = cwq_ref[...], cwk_ref[...]
    cbq, cbk = cbq_ref[...], cbk_ref[...]
    norm_g = ng_ref[...]

    def conv_silu(ref, start, prev_ok, w, b):
        cur = ref[0, pl.ds(start, lc), :]
        prev = ref[0, pl.ds(jnp.maximum(start - SUBLANE, 0), SUBLANE), :]
        prev = jnp.where(prev_ok, prev, 0.0)
        ext = jnp.concatenate([prev, cur], axis=0)
        out = b
        for j in range(ML_CONV):
            off = SUBLANE - (ML_CONV - 1) + j
            out = out + ext[off:off + lc, :] * w[j:j + 1, :]
        return _silu(out)

    def body(c, carry):
        cmat, nvec, m = carry
        start = pl.multiple_of(c * lc, lc)
        prev_ok = c > 0
        qt = conv_silu(q_ref, start, prev_ok, cwq, cbq) * (dk ** -0.5)
        kt = conv_silu(k_ref, start, prev_ok, cwk, cbk)
        vt = v_ref[0, pl.ds(start, lc), :]
        li_row = grow_ref[0, 0, 0:1, pl.ds(start, lc)]
        b_row = grow_ref[0, 0, 1:2, pl.ds(start, lc)]
        li_col = gcol_ref[0, 0, pl.ds(start, lc), 0:1]
        b_col = gcol_ref[0, 0, pl.ds(start, lc), 1:2]
        g = b_row[:, lc - 1:lc]

        a_inter = b_col + m
        dm = jnp.where(causal, b_col - b_row + li_row, -jnp.inf)
        m_t = jnp.maximum(a_inter, jnp.max(dm, axis=-1, keepdims=True))
        w_inter = jnp.exp(a_inter - m_t)
        qb, kb = qt.astype(BF16), kt.astype(BF16)
        s = _dot_nt(qb, kb) * jnp.exp(dm - m_t)
        num = w_inter * _dot(qb, cmat.astype(BF16)) + _dot(s.astype(BF16), vt.astype(BF16))
        den = w_inter * jnp.sum(qt * nvec, axis=-1, keepdims=True) + jnp.sum(s, axis=-1, keepdims=True)
        h = num / jnp.maximum(jnp.abs(den), jnp.exp(-m_t))

        upd = g - b_col + li_col
        m_new = jnp.maximum(g + m, jnp.max(upd, axis=0, keepdims=True))
        decay = jnp.exp(g + m - m_new)
        wk = jnp.exp(upd - m_new) * kt
        cmat = decay * cmat + _dot_tn(wk.astype(BF16), vt.astype(BF16))
        nvec = decay * nvec + jnp.sum(wk, axis=0, keepdims=True)

        hn = h * lax.rsqrt(jnp.mean(h * h, axis=-1, keepdims=True) + NORM_EPS) * norm_g
        og = o_ref[0, pl.ds(start, lc), :]
        y_ref[0, pl.ds(start, lc), :] = (hn * _sigmoid(og)).astype(y_ref.dtype)
        return cmat, nvec, m_new

    init = (jnp.zeros((dk, dv), F32), jnp.zeros((1, dk), F32), jnp.zeros((1, 1), F32))
    lax.fori_loop(0, t // lc, body, init)


def _mlstm(p, gate_rows, gate_cols, conv_w, conv_b, norm_g):
    batch, t, _ = p.shape
    dk, dv = ML_QK_DIM, ML_V_DIM
    nh = ML_HEADS
    return pl.pallas_call(
        functools.partial(_mlstm_kernel, t=t),
        grid=(batch, nh),
        in_specs=[
            pl.BlockSpec((1, t, dk), lambda b, h: (b, 0, P_ML_Q // dk + h)),
            pl.BlockSpec((1, t, dk), lambda b, h: (b, 0, P_ML_K // dk + h)),
            pl.BlockSpec((1, t, dv), lambda b, h: (b, 0, P_ML_V // dv + h)),
            pl.BlockSpec((1, t, dv), lambda b, h: (b, 0, P_ML_O // dv + h)),
            pl.BlockSpec((1, 1, 2, t), lambda b, h: (b, h, 0, 0)),
            pl.BlockSpec((1, 1, t, 2), lambda b, h: (b, h, 0, 0)),
            pl.BlockSpec((ML_CONV, dk), lambda b, h: (0, h)),
            pl.BlockSpec((ML_CONV, dk), lambda b, h: (0, nh + h)),
            pl.BlockSpec((1, dk), lambda b, h: (0, h)),
            pl.BlockSpec((1, dk), lambda b, h: (0, nh + h)),
            pl.BlockSpec((1, dv), lambda b, h: (0, h)),
        ],
        out_specs=pl.BlockSpec((1, t, dv), lambda b, h: (b, 0, h)),
        out_shape=jax.ShapeDtypeStruct((batch, t, ML_WIDTH), BF16),
        compiler_params=_params("parallel", "parallel"),
        name="mlstm_chunkwise",
    )(p, p, p, p, gate_rows, gate_cols, conv_w, conv_w, conv_b, conv_b, norm_g)


RW_CHUNK = 64
RW_TILE = 1024


def _rw_kernel(r_ref, k_ref, v_ref, wl_ref, al_ref, gl_ref,
               rp_ref, kp_ref, vp_ref, wlp_ref, alp_ref, glp_ref,
               mur_ref, muk_ref, muv_ref, muw_ref, mua_ref, mug_ref,
               w0_ref, a0_ref, kkw_ref, kaw_ref, rkw_ref, lnw_ref, lnb_ref,
               wup_ref, aup_ref, gup_ref, y_ref, s_ref, *, tm):
    i = pl.program_id(2)
    lc = RW_CHUNK
    n = RW_HEAD_DIM
    pair = LANE

    @pl.when(i == 0)
    def _():
        s_ref[...] = jnp.zeros_like(s_ref)

    first = lax.broadcasted_iota(jnp.int32, (tm, 1), 0) == 0
    has_prev = i > 0

    def shift(cur_ref, prev_ref, mu_ref):
        cur = cur_ref[0]
        last = jnp.where(has_prev, prev_ref[0][SUBLANE - 1:SUBLANE, :], 0.0)
        prev = jnp.where(first, last, pltpu.roll(cur, 1, axis=0))
        return cur + (prev - cur) * mu_ref[...]

    r = shift(r_ref, rp_ref, mur_ref)
    k = shift(k_ref, kp_ref, muk_ref)
    v = shift(v_ref, vp_ref, muv_ref)
    wl = shift(wl_ref, wlp_ref, muw_ref)
    al = shift(al_ref, alp_ref, mua_ref)
    gl = shift(gl_ref, glp_ref, mug_ref)

    hrow = lax.broadcasted_iota(jnp.int32, (pair, pair), 0)
    hcol = lax.broadcasted_iota(jnp.int32, (pair, pair), 1)
    same_head = jnp.where(hrow // n == hcol // n, 1.0, 0.0).astype(BF16)

    def head_sum(x):
        hi = x.astype(BF16)
        lo = (x - hi.astype(F32)).astype(BF16)
        return _dot(hi, same_head) + _dot(lo, same_head)

    z = w0_ref[...] + _dot(jnp.tanh(wl).astype(BF16), wup_ref[...])
    lw = -jnp.exp(_log_sigmoid(z) - 0.5)
    a = _sigmoid(a0_ref[...] + _dot(al.astype(BF16), aup_ref[...]))
    gate = _dot(_sigmoid(gl).astype(BF16), gup_ref[...])
    kk = k * kkw_ref[...]
    kk = kk / jnp.maximum(jnp.sqrt(head_sum(kk * kk)), 1e-12)
    k = k * (1.0 + (a - 1.0) * kaw_ref[...])
    bb = kk * a
    bonus = head_sum(r * k * rkw_ref[...]) * v

    trow = lax.broadcasted_iota(jnp.int32, (lc, lc), 0)
    tcol = lax.broadcasted_iota(jnp.int32, (lc, lc), 1)
    tri = jnp.where(tcol <= trow, 1.0, 0.0).astype(BF16)
    lw_hi = lw.astype(BF16)
    rem = lw - lw_hi.astype(F32)
    lw_mid = rem.astype(BF16)
    lw_lo = (rem - lw_mid.astype(F32)).astype(BF16)
    nchunk = tm // lc
    cums, ends = [], []
    for c in range(nchunk):
        sl = slice(c * lc, (c + 1) * lc)
        cum_c = _dot(tri, lw_hi[sl]) + (_dot(tri, lw_mid[sl]) + _dot(tri, lw_lo[sl]))
        cums.append(cum_c)
        ends.append(jnp.broadcast_to(cum_c[lc - 1:lc, :], (lc, LANE)))
    cum = jnp.concatenate(cums, axis=0)
    cum_end = jnp.concatenate(ends, axis=0)
    e_inc = jnp.exp(cum)
    e_neg = jnp.exp(-cum)
    tail = jnp.exp(cum_end - cum)
    rt = r * e_inc
    at = kk * jnp.exp(cum - lw)
    kt = k * e_neg
    bt = bb * e_neg
    kh = k * tail
    bh = bb * tail

    head0 = lax.broadcasted_iota(jnp.int32, (1, LANE), 1) < n

    def stack(x, dtype=BF16):
        return jnp.concatenate([jnp.where(head0, x, 0.0), jnp.where(head0, 0.0, x)], axis=0).astype(dtype)

    prow = lax.broadcasted_iota(jnp.int32, (pair, pair), 0)
    pcol = lax.broadcasted_iota(jnp.int32, (pair, pair), 1)
    lower_strict = pcol < prow
    lower_incl = pcol <= prow
    eye = jnp.where(pcol == prow, 1.0, 0.0)

    chunks = range(nchunk)
    rows = [slice(c * lc, (c + 1) * lc) for c in chunks]
    a2 = [stack(at[sl]) for sl in rows]
    r2 = [stack(rt[sl], F32) for sl in rows]
    v2 = [stack(v[sl]) for sl in rows]
    mm = [_dot_nt(jnp.concatenate([a2[c], r2[c].astype(BF16)], axis=0),
                  jnp.concatenate([stack(bt[rows[c]]), stack(kt[rows[c]])], axis=0)) for c in chunks]
    m_ak = [jnp.where(lower_strict, mm[c][:pair, pair:], 0.0).astype(BF16) for c in chunks]
    m_rb = [jnp.where(lower_incl, mm[c][pair:, :pair], 0.0).astype(BF16) for c in chunks]
    m_rk = [jnp.where(lower_incl, mm[c][pair:, pair:], 0.0).astype(BF16) for c in chunks]

    m_ab = [jnp.where(lower_strict, mm[c][:pair, :pair], 0.0) for c in chunks]
    i_plus_m = [(eye + m_ab[c]).astype(BF16) for c in chunks]
    t_inv = [eye - m_ab[c] for c in chunks]
    for _ in range((lc - 1).bit_length() - 1):
        res = [_dot(i_plus_m[c], t_inv[c].astype(BF16)) for c in chunks]
        t_inv = [_dot(t_inv[c].astype(BF16), (2.0 * eye - res[c]).astype(BF16)) for c in chunks]

    x1 = [_dot(m_ak[c], v2[c]) for c in chunks]
    aub = [_dot(t_inv[c].astype(BF16), jnp.concatenate([a2[c], x1[c].astype(BF16)], axis=1)).astype(BF16)
           for c in chunks]
    rb = [_dot(m_rb[c], aub[c]) for c in chunks]
    r_eff = [r2[c] - rb[c][:, :pair] for c in chunks]
    y_loc = [_dot(m_rk[c], v2[c]) - rb[c][:, pair:] for c in chunks]
    r_eff = [(r_eff[c][:lc] + r_eff[c][lc:]).astype(BF16) for c in chunks]
    y_loc = [y_loc[c][:lc] + y_loc[c][lc:] for c in chunks]
    gd = [_dot_tn(aub[c], stack(bh[rows[c]])) for c in chunks]
    g_mat = [(eye * e_inc[(c + 1) * lc - 1:(c + 1) * lc, :] - gd[c][:pair]).astype(BF16) for c in chunks]
    d_mat = [_dot_tn(v2[c], stack(kh[rows[c]])) - gd[c][pair:] for c in chunks]

    s_mat = s_ref[...]
    ys = []
    for c in chunks:
        s_b = s_mat.astype(BF16)
        ys.append(_dot_nt(r_eff[c], s_b) + y_loc[c])
        s_mat = _dot(s_b, g_mat[c]) + d_mat[c]
    s_ref[...] = s_mat

    y = jnp.concatenate(ys, axis=0)
    mean = head_sum(y) * (1.0 / n)
    cen = y - mean
    var = head_sum(cen * cen) * (1.0 / n)
    y = cen * lax.rsqrt(var + RW_LN_EPS) * lnw_ref[...] + lnb_ref[...]
    y_ref[0] = ((y + bonus) * gate).astype(y_ref.dtype)


def _rwkv7(p, mu_pad, w0, a0, k_k, k_a, r_k, ln_w, ln_b, w_up, a_up, g_up_pad):
    batch, t, _ = p.shape
    tm = min(RW_TILE, t)
    nj = RW_WIDTH // LANE
    gw = RW_GATE_LORA_PAD
    rb = tm // SUBLANE

    def cur(col, width=LANE, tiled=True):
        base = col // width
        if tiled:
            return pl.BlockSpec((1, tm, width), lambda b, j, i: (b, i, base + j))
        return pl.BlockSpec((1, tm, width), lambda b, j, i: (b, i, base))

    def prev(col, width=LANE, tiled=True):
        base = col // width
        if tiled:
            return pl.BlockSpec((1, SUBLANE, width), lambda b, j, i: (b, jnp.maximum(i * rb - 1, 0), base + j))
        return pl.BlockSpec((1, SUBLANE, width), lambda b, j, i: (b, jnp.maximum(i * rb - 1, 0), base))

    def murow(col, width=LANE, tiled=True):
        base = col // width
        if tiled:
            return pl.BlockSpec((1, width), lambda b, j, i: (0, base + j))
        return pl.BlockSpec((1, width), lambda b, j, i: (0, base))

    vec = pl.BlockSpec((1, LANE), lambda b, j, i: (0, j))
    up = pl.BlockSpec((RW_LORA, LANE), lambda b, j, i: (0, j))
    gup = pl.BlockSpec((gw, LANE), lambda b, j, i: (0, j))
    return pl.pallas_call(
        functools.partial(_rw_kernel, tm=tm),
        grid=(batch, nj, t // tm),
        in_specs=[
            cur(P_RW_R), cur(P_RW_K), cur(P_RW_V),
            cur(P_RW_WL, tiled=False), cur(P_RW_AL, tiled=False), cur(P_RW_GL, gw, tiled=False),
            prev(P_RW_R), prev(P_RW_K), prev(P_RW_V),
            prev(P_RW_WL, tiled=False), prev(P_RW_AL, tiled=False), prev(P_RW_GL, gw, tiled=False),
            murow(P_RW_R), murow(P_RW_K), murow(P_RW_V),
            murow(P_RW_WL, tiled=False), murow(P_RW_AL, tiled=False), murow(P_RW_GL, gw, tiled=False),
            vec, vec, vec, vec, vec, vec, vec, up, up, gup,
        ],
        out_specs=pl.BlockSpec((1, tm, LANE), lambda b, j, i: (b, i, j)),
        out_shape=jax.ShapeDtypeStruct((batch, t, RW_WIDTH), BF16),
        scratch_shapes=[pltpu.VMEM((LANE, LANE), F32)],
        compiler_params=_params("parallel", "parallel", "arbitrary"),
        name="rwkv7_chunkwise",
    )(p, p, p, p, p, p, p, p, p, p, p, p,
      mu_pad, mu_pad, mu_pad, mu_pad, mu_pad, mu_pad,
      w0, a0, k_k, k_a, r_k, ln_w, ln_b, w_up, a_up, g_up_pad)


def _row(v, width=None):
    v = v.reshape(1, -1).astype(F32)
    if width is not None and v.shape[1] < width:
        v = jnp.pad(v, ((0, 0), (0, width - v.shape[1])))
    return v


def kernel(x, c, ada_w, ada_b, norm1, w_in, fox_f_bias, fox_norm, ml_conv_w, ml_conv_b, ml_i_bias, ml_f_bias,
           ml_norm, rw_mu, rw_w0, rw_w_up, rw_a0, rw_a_up, rw_g_up, rw_k_k, rw_k_a, rw_r_k, rw_ln_w, rw_ln_b,
           w_out, norm2, ffn_gate, ffn_up, ffn_down, final_norm):
    batch, t, d = x.shape
    depth = ada_w.shape[0]
    assert d == D_MODEL and t % ML_CHUNK == 0 and t % RW_CHUNK == 0

    mod = _modulation(c, ada_w, ada_b)
    mod = mod.reshape(depth, batch, 6, 1, d)
    w_in_t = _repack_w_in(w_in)
    w_out_b = _cast_bf16(w_out, "cast_w_out")
    w_gate_b = _cast_bf16(ffn_gate, "cast_ffn_gate")
    w_up_b = _cast_bf16(ffn_up, "cast_ffn_up")
    w_down_b = _cast_bf16(ffn_down, "cast_ffn_down")

    for l in range(depth):
        sh1, sc1, g1, sh2, sc2, g2 = (mod[l, :, i] for i in range(6))

        h = _norm_mod(x, norm1[l], sc1, sh1, BF16)
        p = _matmul_plain(h, w_in_t, l, F32, tm=1024, tn=512)

        fox_bias_row = _row(fox_f_bias[l], LANE)
        ml_bias_row = _row(jnp.concatenate([ml_i_bias[l], ml_f_bias[l]]), LANE)
        cum, li_all, bcs_all = _gates(p, fox_bias_row, ml_bias_row)
        cum_h = cum[:, :, :FOX_HEADS].transpose(0, 2, 1)
        y_fox = _fox_attention(p, cum_h[..., None], fox_norm[l].reshape(FOX_WIDTH, 1))

        li = li_all[:, :, :ML_HEADS]
        bcs = bcs_all[:, :, ML_HEADS:2 * ML_HEADS]
        gate_cols = jnp.stack([li, bcs], axis=-1).transpose(0, 2, 1, 3)
        gate_rows = gate_cols.transpose(0, 1, 3, 2)
        y_ml = _mlstm(p, gate_rows, gate_cols, ml_conv_w[l], _row(ml_conv_b[l]), _row(ml_norm[l]))

        mu_full = jnp.zeros((1, N_IN), F32).at[:, _O_RW:].set(rw_mu[l][None, :])
        mu_pad = _pad_columns(mu_full)
        g_up_pad = jnp.pad(rw_g_up[l], ((RW_GATE_LORA_PAD - RW_GATE_LORA, 0), (0, 0))).astype(BF16)
        y_rw = _rwkv7(p, mu_pad, _row(rw_w0[l]), _row(rw_a0[l]), _row(rw_k_k[l]), _row(rw_k_a[l]),
                      _row(rw_r_k[l]), _row(rw_ln_w[l]), _row(rw_ln_b[l]),
                      rw_w_up[l].astype(BF16), rw_a_up[l].astype(BF16), g_up_pad)

        x = _matmul_mix(y_fox, y_ml, y_rw, w_out_b, l, x, g1, tm=1024, tn=512)

        h = _norm_mod(x, norm2[l], sc2, sh2, BF16)
        act = _matmul_swiglu(h, w_gate_b, w_up_b, l, tm=1024, tn=512)
        x = _matmul_residual(act, w_down_b, l, x, g2, tm=1024, tn=512, tk=D_FF // 2, name="ffn_down")

    zeros = jnp.zeros((batch, 1, d), F32)
    return _norm_mod(x, final_norm, zeros, zeros, F32)
```

```python
import functools

import jax
import jax.numpy as jnp
from jax import lax
from jax.experimental import pallas as pl
from jax.experimental.pallas import tpu as pltpu

F32 = jnp.float32
BF16 = jnp.bfloat16
HIGHEST = lax.Precision.HIGHEST

D_MODEL = 4096
FOX_HEADS, FOX_HEAD_DIM = 12, 128
FOX_WIDTH = FOX_HEADS * FOX_HEAD_DIM
ML_HEADS, ML_QK_DIM, ML_V_DIM = 4, 128, 256
ML_QK_WIDTH = ML_HEADS * ML_QK_DIM
ML_WIDTH = ML_HEADS * ML_V_DIM
ML_CONV = 4
GATE_SOFTCAP = 15.0
RW_HEADS, RW_HEAD_DIM = 24, 64
RW_WIDTH = RW_HEADS * RW_HEAD_DIM
RW_LORA = 128
RW_GATE_LORA = 480
RW_GATE_LORA_PAD = 512
RW_LN_EPS = 64e-5
D_FF = 11008
NORM_EPS = 1e-6
LOG2_E = 1.4426950408889634

LANE = 128
SUBLANE = 8
VMEM_LIMIT = 56 * 1024 * 1024

_O_FOX_Q, _O_FOX_K, _O_FOX_V, _O_FOX_F = 0, 1536, 3072, 4608
_O_ML = 4620
_O_ML_QK, _O_ML_V, _O_ML_I, _O_ML_F, _O_ML_O = _O_ML, _O_ML + 1024, _O_ML + 2048, _O_ML + 2052, _O_ML + 2056
_O_RW = 7700
_O_RW_R, _O_RW_K, _O_RW_V = _O_RW, _O_RW + 1536, _O_RW + 3072
_O_RW_WL, _O_RW_AL, _O_RW_GL = _O_RW + 4608, _O_RW + 4736, _O_RW + 4864
N_IN = 13044
P_FOX_Q, P_FOX_K, P_FOX_V = 0, 1536, 3072
P_ML_V, P_ML_O, P_ML_Q, P_ML_K = 4608, 5632, 6656, 7168
P_RW_R, P_RW_K, P_RW_V = 7680, 9216, 10752
P_RW_GL, P_RW_WL, P_RW_AL = 12288, 12800, 12928
P_FOX_F, P_ML_IF = 13056, 13184
N_PAD = 13312
_SECTIONS = (
    (P_FOX_Q, _O_FOX_Q, 3 * FOX_WIDTH),
    (P_ML_V, _O_ML_V, ML_WIDTH),
    (P_ML_O, _O_ML_O, ML_WIDTH),
    (P_ML_Q, _O_ML_QK, 2 * ML_QK_WIDTH),
    (P_RW_R, _O_RW_R, 3 * RW_WIDTH),
    (P_RW_GL + RW_GATE_LORA_PAD - RW_GATE_LORA, _O_RW_GL, RW_GATE_LORA),
    (P_RW_WL, _O_RW_WL, 2 * RW_LORA),
    (P_FOX_F, _O_FOX_F, FOX_HEADS),
    (P_ML_IF, _O_ML_I, 2 * ML_HEADS),
)


def _pad_columns(w):
    parts, pos = [], 0
    for p_start, o_start, width in _SECTIONS:
        if p_start > pos:
            parts.append(jnp.zeros(w.shape[:-1] + (p_start - pos,), w.dtype))
        parts.append(w[..., o_start:o_start + width])
        pos = p_start + width
    if pos < N_PAD:
        parts.append(jnp.zeros(w.shape[:-1] + (N_PAD - pos,), w.dtype))
    return jnp.concatenate(parts, axis=-1)


def _params(*sem):
    return pltpu.CompilerParams(dimension_semantics=sem, vmem_limit_bytes=VMEM_LIMIT)


def _repack_table():
    src, lo, hi = [0] * (N_PAD // LANE), [0] * (N_PAD // LANE), [0] * (N_PAD // LANE)
    for p_start, o_start, width in _SECTIONS:
        first = p_start // LANE
        last = (p_start + width - 1) // LANE
        for tile in range(first, last + 1):
            base = tile * LANE
            src[tile] = o_start + (base - p_start)
            lo[tile] = max(p_start - base, 0)
            hi[tile] = min(p_start + width - base, LANE)
            assert 0 <= src[tile] and src[tile] + LANE <= N_IN
    return (jnp.asarray(src, jnp.int32), jnp.asarray(lo, jnp.int32), jnp.asarray(hi, jnp.int32))


def _repack_kernel(src_ref, lo_ref, hi_ref, w_ref, o_ref):
    i = pl.program_id(0)
    row = lax.broadcasted_iota(jnp.int32, (LANE, 1), 0)
    valid = (row >= lo_ref[i]) & (row < hi_ref[i])
    for layer in range(o_ref.shape[0]):
        o_ref[layer] = jnp.where(valid, w_ref[:, layer, :], 0.0).astype(o_ref.dtype)


def _repack_w_in(w_in):
    depth, kdim, _ = w_in.shape
    w_t = jnp.transpose(w_in, (2, 0, 1))
    src, lo, hi = _repack_table()
    return pl.pallas_call(
        _repack_kernel,
        grid_spec=pltpu.PrefetchScalarGridSpec(
            num_scalar_prefetch=3,
            grid=(N_PAD // LANE,),
            in_specs=[pl.BlockSpec((pl.Element(LANE), pl.Element(depth), pl.Element(kdim)),
                                   lambda i, src, lo, hi: (src[i], 0, 0))],
            out_specs=pl.BlockSpec((depth, LANE, kdim), lambda i, src, lo, hi: (0, i, 0)),
        ),
        out_shape=jax.ShapeDtypeStruct((depth, N_PAD, kdim), BF16),
        compiler_params=_params("parallel"),
        name="repack_w_in",
    )(src, lo, hi, w_t)


def _log_sigmoid(x):
    return jnp.minimum(x, 0.0) - jnp.log1p(jnp.exp(-jnp.abs(x)))


def _sigmoid(x):
    return 1.0 / (1.0 + jnp.exp(-x))


def _silu(x):
    return x * _sigmoid(x)


def _dot(a, b, precision=None):
    return jnp.dot(a, b, preferred_element_type=F32, precision=precision)


def _dot_nt(a, b, precision=None):
    return lax.dot_general(a, b, (((1,), (1,)), ((), ())), preferred_element_type=F32, precision=precision)


def _dot_tn(a, b, precision=None):
    return lax.dot_general(a, b, (((0,), (0,)), ((), ())), preferred_element_type=F32, precision=precision)


def _mod_kernel(c_ref, w_ref, b_ref, o_ref):
    a = _silu(c_ref[...]).astype(BF16)
    o_ref[0] = _dot(a, w_ref[0].astype(BF16)) + b_ref[0]


def _modulation(c, ada_w, ada_b):
    depth, d, n = ada_w.shape
    batch = c.shape[0]
    rows = -(-batch // SUBLANE) * SUBLANE
    c_pad = jnp.zeros((rows, d), F32).at[:batch].set(c)
    tn = 512
    out = pl.pallas_call(
        _mod_kernel,
        grid=(depth, n // tn),
        in_specs=[
            pl.BlockSpec((rows, d), lambda l, j: (0, 0)),
            pl.BlockSpec((1, d, tn), lambda l, j: (l, 0, j)),
            pl.BlockSpec((1, 1, tn), lambda l, j: (l, 0, j)),
        ],
        out_specs=pl.BlockSpec((1, rows, tn), lambda l, j: (l, 0, j)),
        out_shape=jax.ShapeDtypeStruct((depth, rows, n), F32),
        compiler_params=_params("parallel", "parallel"),
        name="adaln_modulation",
    )(c_pad, ada_w, ada_b.reshape(depth, 1, n))
    return out[:, :batch]


def _norm_kernel(x_ref, g_ref, sc_ref, sh_ref, o_ref):
    x = x_ref[0]
    y = x * lax.rsqrt(jnp.mean(x * x, axis=-1, keepdims=True) + NORM_EPS)
    y = (y * g_ref[...]) * (1.0 + sc_ref[0]) + sh_ref[0]
    o_ref[0] = y.astype(o_ref.dtype)


def _norm_mod(x, g, scale, shift, out_dtype):
    batch, t, d = x.shape
    tm = min(512, t)
    return pl.pallas_call(
        _norm_kernel,
        grid=(batch, t // tm),
        in_specs=[
            pl.BlockSpec((1, tm, d), lambda b, i: (b, i, 0)),
            pl.BlockSpec((1, d), lambda b, i: (0, 0)),
            pl.BlockSpec((1, 1, d), lambda b, i: (b, 0, 0)),
            pl.BlockSpec((1, 1, d), lambda b, i: (b, 0, 0)),
        ],
        out_specs=pl.BlockSpec((1, tm, d), lambda b, i: (b, i, 0)),
        out_shape=jax.ShapeDtypeStruct((batch, t, d), out_dtype),
        compiler_params=_params("parallel", "parallel"),
        name="rmsnorm_modulate",
    )(x, g.reshape(1, d), scale, shift)


def _mm_plain_kernel(a_ref, wt_ref, o_ref):
    o_ref[0] = _dot_nt(a_ref[0], wt_ref[0]).astype(o_ref.dtype)


def _mm_swiglu_kernel(a_ref, wg_ref, wu_ref, o_ref):
    a = a_ref[0]
    gate = _dot(a, wg_ref[0])
    up = _dot(a, wu_ref[0])
    o_ref[0] = (_silu(gate) * up).astype(o_ref.dtype)


def _mm_residual_kernel(a_ref, w_ref, x_ref, g_ref, o_ref, acc_ref, *, nk):
    k = pl.program_id(3)

    @pl.when(k == 0)
    def _():
        acc_ref[...] = jnp.zeros_like(acc_ref)

    acc_ref[...] += _dot(a_ref[0], w_ref[0])

    @pl.when(k == nk - 1)
    def _():
        o_ref[0] = x_ref[0] + g_ref[0] * acc_ref[...]


def _mm_mix_kernel(yf_ref, ym_ref, yr_ref, w_ref, x_ref, g_ref, o_ref):
    k1, k2 = FOX_WIDTH, FOX_WIDTH + ML_WIDTH
    acc = _dot(yf_ref[0], w_ref[0, :k1, :])
    acc += _dot(ym_ref[0], w_ref[0, k1:k2, :])
    acc += _dot(yr_ref[0], w_ref[0, k2:, :])
    o_ref[0] = x_ref[0] + g_ref[0] * acc


def _cast_kernel(w_ref, o_ref):
    o_ref[...] = w_ref[...].astype(o_ref.dtype)


def _cast_bf16(w, name):
    depth, r, c = w.shape
    tr = 256
    return pl.pallas_call(
        _cast_kernel,
        grid=(depth, r // tr),
        in_specs=[pl.BlockSpec((1, tr, c), lambda l, i: (l, i, 0))],
        out_specs=pl.BlockSpec((1, tr, c), lambda l, i: (l, i, 0)),
        out_shape=jax.ShapeDtypeStruct(w.shape, BF16),
        compiler_params=_params("parallel", "parallel"),
        name=name,
    )(w)


def _matmul_plain(a, w_t, layer, out_dtype, tm, tn):
    batch, t, kdim = a.shape
    n = w_t.shape[1]
    tm = min(tm, t)
    return pl.pallas_call(
        _mm_plain_kernel,
        grid=(batch, t // tm, n // tn),
        in_specs=[
            pl.BlockSpec((1, tm, kdim), lambda b, i, j: (b, i, 0)),
            pl.BlockSpec((1, tn, kdim), lambda b, i, j: (layer, j, 0)),
        ],
        out_specs=pl.BlockSpec((1, tm, tn), lambda b, i, j: (b, i, j)),
        out_shape=jax.ShapeDtypeStruct((batch, t, n), out_dtype),
        compiler_params=_params("parallel", "parallel", "arbitrary"),
        name="input_projection",
    )(a, w_t)


def _matmul_swiglu(a, wg, wu, layer, tm, tn):
    batch, t, kdim = a.shape
    n = wg.shape[2]
    tm = min(tm, t)
    w_spec = pl.BlockSpec((1, kdim, tn), lambda b, i, j: (layer, 0, j))
    return pl.pallas_call(
        _mm_swiglu_kernel,
        grid=(batch, t // tm, pl.cdiv(n, tn)),
        in_specs=[pl.BlockSpec((1, tm, kdim), lambda b, i, j: (b, i, 0)), w_spec, w_spec],
        out_specs=pl.BlockSpec((1, tm, tn), lambda b, i, j: (b, i, j)),
        out_shape=jax.ShapeDtypeStruct((batch, t, n), BF16),
        compiler_params=_params("parallel", "parallel", "arbitrary"),
        name="ffn_gate_up_swiglu",
    )(a, wg, wu)


def _matmul_residual(a, w, layer, x, gate, tm, tn, tk, name):
    batch, t, kdim = a.shape
    n = w.shape[2]
    tm = min(tm, t)
    assert kdim % tk == 0 and tk % LANE == 0
    nk = kdim // tk
    return pl.pallas_call(
        functools.partial(_mm_residual_kernel, nk=nk),
        grid=(batch, t // tm, n // tn, nk),
        in_specs=[
            pl.BlockSpec((1, tm, tk), lambda b, i, j, k: (b, i, k)),
            pl.BlockSpec((1, tk, tn), lambda b, i, j, k: (layer, k, j)),
            pl.BlockSpec((1, tm, tn), lambda b, i, j, k: (b, i, j)),
            pl.BlockSpec((1, 1, tn), lambda b, i, j, k: (b, 0, j)),
        ],
        out_specs=pl.BlockSpec((1, tm, tn), lambda b, i, j, k: (b, i, j)),
        out_shape=jax.ShapeDtypeStruct((batch, t, n), F32),
        scratch_shapes=[pltpu.VMEM((tm, tn), F32)],
        compiler_params=_params("parallel", "parallel", "parallel", "arbitrary"),
        name=name,
    )(a, w, x, gate)


def _matmul_mix(y_fox, y_ml, y_rw, w, layer, x, gate, tm, tn):
    batch, t, _ = x.shape
    kdim, n = w.shape[1], w.shape[2]
    tm = min(tm, t)
    act = lambda width: pl.BlockSpec((1, tm, width), lambda b, i, j: (b, i, 0))
    return pl.pallas_call(
        _mm_mix_kernel,
        grid=(batch, t // tm, n // tn),
        in_specs=[
            act(FOX_WIDTH), act(ML_WIDTH), act(RW_WIDTH),
            pl.BlockSpec((1, kdim, tn), lambda b, i, j: (layer, 0, j)),
            pl.BlockSpec((1, tm, tn), lambda b, i, j: (b, i, j)),
            pl.BlockSpec((1, 1, tn), lambda b, i, j: (b, 0, j)),
        ],
        out_specs=pl.BlockSpec((1, tm, tn), lambda b, i, j: (b, i, j)),
        out_shape=jax.ShapeDtypeStruct((batch, t, n), F32),
        compiler_params=_params("parallel", "parallel", "arbitrary"),
        name="output_projection",
    )(y_fox, y_ml, y_rw, w, x, gate)


ML_CHUNK = 128
_CUM_ROWS = 256


def _gates_kernel(ff_ref, mif_ref, fb_ref, mb_ref, cum_ref, li_ref, bcs_ref, *, t):
    r = _CUM_ROWS if t % _CUM_ROWS == 0 else t
    row = lax.broadcasted_iota(jnp.int32, (r, r), 0)
    col = lax.broadcasted_iota(jnp.int32, (r, r), 1)
    tril = (col <= row).astype(F32)
    fb = fb_ref[...]

    def cum_body(i, carry):
        start = pl.multiple_of(i * r, r)
        lg = _log_sigmoid(ff_ref[0, pl.ds(start, r), :] + fb)
        cs = _dot(tril, lg, HIGHEST) + carry
        cum_ref[0, pl.ds(start, r), :] = cs
        return cs[r - 1:r, :]

    lax.fori_loop(0, t // r, cum_body, jnp.zeros((1, LANE), F32))

    lc = ML_CHUNK
    rowc = lax.broadcasted_iota(jnp.int32, (lc, lc), 0)
    colc = lax.broadcasted_iota(jnp.int32, (lc, lc), 1)
    trilc = (colc <= rowc).astype(F32)
    mb = mb_ref[...]

    def ml_body(i, carry):
        start = pl.multiple_of(i * lc, lc)
        z = mif_ref[0, pl.ds(start, lc), :] + mb
        capped = GATE_SOFTCAP * jnp.tanh(z / GATE_SOFTCAP)
        li_ref[0, pl.ds(start, lc), :] = capped
        bcs_ref[0, pl.ds(start, lc), :] = _dot(trilc, _log_sigmoid(capped), HIGHEST)
        return carry

    lax.fori_loop(0, t // lc, ml_body, 0)


def _gates(p, fox_bias_row, ml_bias_row):
    batch, t, _ = p.shape
    blk = lambda col: pl.BlockSpec((1, t, LANE), lambda b: (b, 0, col // LANE))
    row = pl.BlockSpec((1, LANE), lambda b: (0, 0))
    out = pl.BlockSpec((1, t, LANE), lambda b: (b, 0, 0))
    shp = jax.ShapeDtypeStruct((batch, t, LANE), F32)
    return pl.pallas_call(
        functools.partial(_gates_kernel, t=t),
        grid=(batch,),
        in_specs=[blk(P_FOX_F), blk(P_ML_IF), row, row],
        out_specs=[out, out, out],
        out_shape=[shp, shp, shp],
        compiler_params=_params("parallel"),
        name="gate_cumsums",
    )(p, p, fox_bias_row, ml_bias_row)


FOX_TQ = 1024
FOX_TK = 1024


def _fox_kernel(q_ref, k_ref, v_ref, fk_ref, g_ref, o_ref, kb_ref, vt_ref, *, t, tq, tk):
    i = pl.program_id(2)
    dh = FOX_HEAD_DIM

    @pl.when(i == 0)
    def _():
        for c in range(t // tk):
            rows = slice(c * tk, (c + 1) * tk)
            kb_ref[rows, :] = k_ref[0, rows, :].astype(BF16)
            vt_ref[:, rows] = jnp.transpose(v_ref[0, rows, :]).astype(BF16)

    q = (q_ref[0] * (dh ** -0.5 * LOG2_E)).astype(BF16)
    q_pos = i * tq + lax.broadcasted_iota(jnp.int32, (1, tq), 1)

    def step(j, carry, masked):
        m, l, acc = carry
        start = pl.multiple_of(j * tk, tk)
        fk = fk_ref[0, 0, pl.ds(start, tk), :] * LOG2_E
        s = _dot_nt(kb_ref[pl.ds(start, tk), :], q) - fk
        if masked:
            k_pos = start + lax.broadcasted_iota(jnp.int32, (tk, 1), 0)
            s = jnp.where(k_pos <= q_pos, s, -jnp.inf)
        m_new = jnp.maximum(m, jnp.max(s, axis=0, keepdims=True))
        p = jnp.exp2(s - m_new)
        alpha = jnp.exp2(m - m_new)
        l = alpha * l + jnp.sum(p, axis=0, keepdims=True)
        acc = alpha * acc + _dot(vt_ref[:, pl.ds(start, tk)], p.astype(BF16))
        return m_new, l, acc

    per_tile = tq // tk
    carry = (jnp.full((1, tq), -1e30, F32), jnp.zeros((1, tq), F32), jnp.zeros((dh, tq), F32))
    carry = lax.fori_loop(0, i * per_tile, functools.partial(step, masked=False), carry)
    _, l, acc = lax.fori_loop(i * per_tile, (i + 1) * per_tile, functools.partial(step, masked=True), carry)
    o = acc / l
    o = o * lax.rsqrt(jnp.mean(o * o, axis=0, keepdims=True) + NORM_EPS) * g_ref[...]
    o_ref[0] = jnp.transpose(o).astype(o_ref.dtype)


def _fox_attention(p, cum_col, out_g_col):
    batch, t, _ = p.shape
    tq = min(FOX_TQ, t)
    tk = min(FOX_TK, tq)
    dh = FOX_HEAD_DIM
    return pl.pallas_call(
        functools.partial(_fox_kernel, t=t, tq=tq, tk=tk),
        grid=(batch, FOX_HEADS, t // tq),
        in_specs=[
            pl.BlockSpec((1, tq, dh), lambda b, h, i: (b, i, P_FOX_Q // dh + h)),
            pl.BlockSpec((1, t, dh), lambda b, h, i: (b, 0, P_FOX_K // dh + h)),
            pl.BlockSpec((1, t, dh), lambda b, h, i: (b, 0, P_FOX_V // dh + h)),
            pl.BlockSpec((1, 1, t, 1), lambda b, h, i: (b, h, 0, 0)),
            pl.BlockSpec((dh, 1), lambda b, h, i: (h, 0)),
        ],
        out_specs=pl.BlockSpec((1, tq, dh), lambda b, h, i: (b, i, h)),
        out_shape=jax.ShapeDtypeStruct((batch, t, FOX_WIDTH), BF16),
        scratch_shapes=[pltpu.VMEM((t, dh), BF16), pltpu.VMEM((dh, t), BF16)],
        compiler_params=_params("parallel", "parallel", "arbitrary"),
        name="fox_attention",
    )(p, p, p, cum_col, out_g_col)


def _mlstm_kernel(q_ref, k_ref, v_ref, o_ref, grow_ref, gcol_ref, cwq_ref, cwk_ref, cbq_ref, cbk_ref,
                  ng_ref, y_ref, *, t):
    lc = ML_CHUNK
    dk, dv = ML_QK_DIM, ML_V_DIM
    row = lax.broadcasted_iota(jnp.int32, (lc, lc), 0)
    col = lax.broadcasted_iota(jnp.int32, (lc, lc), 1)
    causal = col <= row
    cwq, cwk = cwq_ref[...], cwk_ref[...]
    cbq, cbk = cbq_ref[...], cbk_ref[...]
    norm_g = ng_ref[...]

    def conv_silu(ref, start, prev_ok, w, b):
        cur = ref[0, pl.ds(start, lc), :]
        prev = ref[0, pl.ds(jnp.maximum(start - SUBLANE, 0), SUBLANE), :]
        prev = jnp.where(prev_ok, prev, 0.0)
        ext = jnp.concatenate([prev, cur], axis=0)
        out = b
        for j in range(ML_CONV):
            off = SUBLANE - (ML_CONV - 1) + j
            out = out + ext[off:off + lc, :] * w[j:j + 1, :]
        return _silu(out)

    def body(c, carry):
        cmat, nvec, m = carry
        start = pl.multiple_of(c * lc, lc)
        prev_ok = c > 0
        qt = conv_silu(q_ref, start, prev_ok, cwq, cbq) * (dk ** -0.5)
        kt = conv_silu(k_ref, start, prev_ok, cwk, cbk)
        vt = v_ref[0, pl.ds(start, lc), :]
        li_row = grow_ref[0, 0, 0:1, pl.ds(start, lc)]
        b_row = grow_ref[0, 0, 1:2, pl.ds(start, lc)]
        li_col = gcol_ref[0, 0, pl.ds(start, lc), 0:1]
        b_col = gcol_ref[0, 0, pl.ds(start, lc), 1:2]
        g = b_row[:, lc - 1:lc]

        a_inter = b_col + m
        dm = jnp.where(causal, b_col - b_row + li_row, -jnp.inf)
        m_t = jnp.maximum(a_inter, jnp.max(dm, axis=-1, keepdims=True))
        w_inter = jnp.exp(a_inter - m_t)
        qb, kb = qt.astype(BF16), kt.astype(BF16)
        s = _dot_nt(qb, kb) * jnp.exp(dm - m_t)
        num = w_inter * _dot(qb, cmat.astype(BF16)) + _dot(s.astype(BF16), vt.astype(BF16))
        den = w_inter * jnp.sum(qt * nvec, axis=-1, keepdims=True) + jnp.sum(s, axis=-1, keepdims=True)
        h = num / jnp.maximum(jnp.abs(den), jnp.exp(-m_t))

        upd = g - b_col + li_col
        m_new = jnp.maximum(g + m, jnp.max(upd, axis=0, keepdims=True))
        decay = jnp.exp(g + m - m_new)
        wk = jnp.exp(upd - m_new) * kt
        cmat = decay * cmat + _dot_tn(wk.astype(BF16), vt.astype(BF16))
        nvec = decay * nvec + jnp.sum(wk, axis=0, keepdims=True)

        hn = h * lax.rsqrt(jnp.mean(h * h, axis=-1, keepdims=True) + NORM_EPS) * norm_g
        og = o_ref[0, pl.ds(start, lc), :]
        y_ref[0, pl.ds(start, lc), :] = (hn * _sigmoid(og)).astype(y_ref.dtype)
        return cmat, nvec, m_new

    init = (jnp.zeros((dk, dv), F32), jnp.zeros((1, dk), F32), jnp.zeros((1, 1), F32))
    lax.fori_loop(0, t // lc, body, init)


def _mlstm(p, gate_rows, gate_cols, conv_w, conv_b, norm_g):
    batch, t, _ = p.shape
    dk, dv = ML_QK_DIM, ML_V_DIM
    nh = ML_HEADS
    return pl.pallas_call(
        functools.partial(_mlstm_kernel, t=t),
        grid=(batch, nh),
        in_specs=[
            pl.BlockSpec((1, t, dk), lambda b, h: (b, 0, P_ML_Q // dk + h)),
            pl.BlockSpec((1, t, dk), lambda b, h: (b, 0, P_ML_K // dk + h)),
            pl.BlockSpec((1, t, dv), lambda b, h: (b, 0, P_ML_V // dv + h)),
            pl.BlockSpec((1, t, dv), lambda b, h: (b, 0, P_ML_O // dv + h)),
            pl.BlockSpec((1, 1, 2, t), lambda b, h: (b, h, 0, 0)),
            pl.BlockSpec((1, 1, t, 2), lambda b, h: (b, h, 0, 0)),
            pl.BlockSpec((ML_CONV, dk), lambda b, h: (0, h)),
            pl.BlockSpec((ML_CONV, dk), lambda b, h: (0, nh + h)),
            pl.BlockSpec((1, dk), lambda b, h: (0, h)),
            pl.BlockSpec((1, dk), lambda b, h: (0, nh + h)),
            pl.BlockSpec((1, dv), lambda b, h: (0, h)),
        ],
        out_specs=pl.BlockSpec((1, t, dv), lambda b, h: (b, 0, h)),
        out_shape=jax.ShapeDtypeStruct((batch, t, ML_WIDTH), BF16),
        compiler_params=_params("parallel", "parallel"),
        name="mlstm_chunkwise",
    )(p, p, p, p, gate_rows, gate_cols, conv_w, conv_w, conv_b, conv_b, norm_g)


RW_CHUNK = 64
RW_TILE = 1024


def _rw_kernel(r_ref, k_ref, v_ref, wl_ref, al_ref, gl_ref,
               rp_ref, kp_ref, vp_ref, wlp_ref, alp_ref, glp_ref,
               mur_ref, muk_ref, muv_ref, muw_ref, mua_ref, mug_ref,
               w0_ref, a0_ref, kkw_ref, kaw_ref, rkw_ref, lnw_ref, lnb_ref,
               wup_ref, aup_ref, gup_ref, y_ref, s_ref, *, tm):
    i = pl.program_id(2)
    lc = RW_CHUNK
    n = RW_HEAD_DIM
    pair = LANE

    @pl.when(i == 0)
    def _():
        s_ref[...] = jnp.zeros_like(s_ref)

    first = lax.broadcasted_iota(jnp.int32, (tm, 1), 0) == 0
    has_prev = i > 0

    def shift(cur_ref, prev_ref, mu_ref):
        cur = cur_ref[0]
        last = jnp.where(has_prev, prev_ref[0][SUBLANE - 1:SUBLANE, :], 0.0)
        prev = jnp.where(first, last, pltpu.roll(cur, 1, axis=0))
        return cur + (prev - cur) * mu_ref[...]

    r = shift(r_ref, rp_ref, mur_ref)
    k = shift(k_ref, kp_ref, muk_ref)
    v = shift(v_ref, vp_ref, muv_ref)
    wl = shift(wl_ref, wlp_ref, muw_ref)
    al = shift(al_ref, alp_ref, mua_ref)
    gl = shift(gl_ref, glp_ref, mug_ref)

    hrow = lax.broadcasted_iota(jnp.int32, (pair, pair), 0)
    hcol = lax.broadcasted_iota(jnp.int32, (pair, pair), 1)
    same_head = jnp.where(hrow // n == hcol // n, 1.0, 0.0).astype(BF16)

    def head_sum(x):
        hi = x.astype(BF16)
        lo = (x - hi.astype(F32)).astype(BF16)
        return _dot(hi, same_head) + _dot(lo, same_head)

    z = w0_ref[...] + _dot(jnp.tanh(wl).astype(BF16), wup_ref[...])
    lw = -jnp.exp(_log_sigmoid(z) - 0.5)
    a = _sigmoid(a0_ref[...] + _dot(al.astype(BF16), aup_ref[...]))
    gate = _dot(_sigmoid(gl).astype(BF16), gup_ref[...])
    kk = k * kkw_ref[...]
    kk = kk / jnp.maximum(jnp.sqrt(head_sum(kk * kk)), 1e-12)
    k = k * (1.0 + (a - 1.0) * kaw_ref[...])
    bb = kk * a
    bonus = head_sum(r * k * rkw_ref[...]) * v

    trow = lax.broadcasted_iota(jnp.int32, (lc, lc), 0)
    tcol = lax.broadcasted_iota(jnp.int32, (lc, lc), 1)
    tri = jnp.where(tcol <= trow, 1.0, 0.0).astype(BF16)
    lw_hi = lw.astype(BF16)
    rem = lw - lw_hi.astype(F32)
    lw_mid = rem.astype(BF16)
    lw_lo = (rem - lw_mid.astype(F32)).astype(BF16)
    nchunk = tm // lc
    cums, ends = [], []
    for c in range(nchunk):
        sl = slice(c * lc, (c + 1) * lc)
        cum_c = _dot(tri, lw_hi[sl]) + (_dot(tri, lw_mid[sl]) + _dot(tri, lw_lo[sl]))
        cums.append(cum_c)
        ends.append(jnp.broadcast_to(cum_c[lc - 1:lc, :], (lc, LANE)))
    cum = jnp.concatenate(cums, axis=0)
    cum_end = jnp.concatenate(ends, axis=0)
    e_inc = jnp.exp(cum)
    e_neg = jnp.exp(-cum)
    tail = jnp.exp(cum_end - cum)
    rt = r * e_inc
    at = kk * jnp.exp(cum - lw)
    kt = k * e_neg
    bt = bb * e_neg
    kh = k * tail
    bh = bb * tail

    head0 = lax.broadcasted_iota(jnp.int32, (1, LANE), 1) < n

    def stack(x, dtype=BF16):
        return jnp.concatenate([jnp.where(head0, x, 0.0), jnp.where(head0, 0.0, x)], axis=0).astype(dtype)

    prow = lax.broadcasted_iota(jnp.int32, (pair, pair), 0)
    pcol = lax.broadcasted_iota(jnp.int32, (pair, pair), 1)
    lower_strict = pcol < prow
    lower_incl = pcol <= prow
    eye = jnp.where(pcol == prow, 1.0, 0.0)

    chunks = range(nchunk)
    rows = [slice(c * lc, (c + 1) * lc) for c in chunks]
    a2 = [stack(at[sl]) for sl in rows]
    r2 = [stack(rt[sl], F32) for sl in rows]
    v2 = [stack(v[sl]) for sl in rows]
    mm = [_dot_nt(jnp.concatenate([a2[c], r2[c].astype(BF16)], axis=0),
                  jnp.concatenate([stack(bt[rows[c]]), stack(kt[rows[c]])], axis=0)) for c in chunks]
    m_ak = [jnp.where(lower_strict, mm[c][:pair, pair:], 0.0).astype(BF16) for c in chunks]
    m_rb = [jnp.where(lower_incl, mm[c][pair:, :pair], 0.0).astype(BF16) for c in chunks]
    m_rk = [jnp.where(lower_incl, mm[c][pair:, pair:], 0.0).astype(BF16) for c in chunks]

    m_ab = [jnp.where(lower_strict, mm[c][:pair, :pair], 0.0) for c in chunks]
    i_plus_m = [(eye + m_ab[c]).astype(BF16) for c in chunks]
    t_inv = [eye - m_ab[c] for c in chunks]
    for _ in range((lc - 1).bit_length() - 1):
        res = [_dot(i_plus_m[c], t_inv[c].astype(BF16)) for c in chunks]
        t_inv = [_dot(t_inv[c].astype(BF16), (2.0 * eye - res[c]).astype(BF16)) for c in chunks]

    x1 = [_dot(m_ak[c], v2[c]) for c in chunks]
    aub = [_dot(t_inv[c].astype(BF16), jnp.concatenate([a2[c], x1[c].astype(BF16)], axis=1)).astype(BF16)
           for c in chunks]
    rb = [_dot(m_rb[c], aub[c]) for c in chunks]
    r_eff = [r2[c] - rb[c][:, :pair] for c in chunks]
    y_loc = [_dot(m_rk[c], v2[c]) - rb[c][:, pair:] for c in chunks]
    r_eff = [(r_eff[c][:lc] + r_eff[c][lc:]).astype(BF16) for c in chunks]
    y_loc = [y_loc[c][:lc] + y_loc[c][lc:] for c in chunks]
    gd = [_dot_tn(aub[c], stack(bh[rows[c]])) for c in chunks]
    g_mat = [(eye * e_inc[(c + 1) * lc - 1:(c + 1) * lc, :] - gd[c][:pair]).astype(BF16) for c in chunks]
    d_mat = [_dot_tn(v2[c], stack(kh[rows[c]])) - gd[c][pair:] for c in chunks]

    s_mat = s_ref[...]
    ys = []
    for c in chunks:
        s_b = s_mat.astype(BF16)
        ys.append(_dot_nt(r_eff[c], s_b) + y_loc[c])
        s_mat = _dot(s_b, g_mat[c]) + d_mat[c]
    s_ref[...] = s_mat

    y = jnp.concatenate(ys, axis=0)
    mean = head_sum(y) * (1.0 / n)
    cen = y - mean
    var = head_sum(cen * cen) * (1.0 / n)
    y = cen * lax.rsqrt(var + RW_LN_EPS) * lnw_ref[...] + lnb_ref[...]
    y_ref[0] = ((y + bonus) * gate).astype(y_ref.dtype)


def _rwkv7(p, mu_pad, w0, a0, k_k, k_a, r_k, ln_w, ln_b, w_up, a_up, g_up_pad):
    batch, t, _ = p.shape
    tm = min(RW_TILE, t)
    nj = RW_WIDTH // LANE
    gw = RW_GATE_LORA_PAD
    rb = tm // SUBLANE

    def cur(col, width=LANE, tiled=True):
        base = col // width
        if tiled:
            return pl.BlockSpec((1, tm, width), lambda b, j, i: (b, i, base + j))
        return pl.BlockSpec((1, tm, width), lambda b, j, i: (b, i, base))

    def prev(col, width=LANE, tiled=True):
        base = col // width
        if tiled:
            return pl.BlockSpec((1, SUBLANE, width), lambda b, j, i: (b, jnp.maximum(i * rb - 1, 0), base + j))
        return pl.BlockSpec((1, SUBLANE, width), lambda b, j, i: (b, jnp.maximum(i * rb - 1, 0), base))

    def murow(col, width=LANE, tiled=True):
        base = col // width
        if tiled:
            return pl.BlockSpec((1, width), lambda b, j, i: (0, base + j))
        return pl.BlockSpec((1, width), lambda b, j, i: (0, base))

    vec = pl.BlockSpec((1, LANE), lambda b, j, i: (0, j))
    up = pl.BlockSpec((RW_LORA, LANE), lambda b, j, i: (0, j))
    gup = pl.BlockSpec((gw, LANE), lambda b, j, i: (0, j))
    return pl.pallas_call(
        functools.partial(_rw_kernel, tm=tm),
        grid=(batch, nj, t // tm),
        in_specs=[
            cur(P_RW_R), cur(P_RW_K), cur(P_RW_V),
            cur(P_RW_WL, tiled=False), cur(P_RW_AL, tiled=False), cur(P_RW_GL, gw, tiled=False),
            prev(P_RW_R), prev(P_RW_K), prev(P_RW_V),
            prev(P_RW_WL, tiled=False), prev(P_RW_AL, tiled=False), prev(P_RW_GL, gw, tiled=False),
            murow(P_RW_R), murow(P_RW_K), murow(P_RW_V),
            murow(P_RW_WL, tiled=False), murow(P_RW_AL, tiled=False), murow(P_RW_GL, gw, tiled=False),
            vec, vec, vec, vec, vec, vec, vec, up, up, gup,
        ],
        out_specs=pl.BlockSpec((1, tm, LANE), lambda b, j, i: (b, i, j)),
        out_shape=jax.ShapeDtypeStruct((batch, t, RW_WIDTH), BF16),
        scratch_shapes=[pltpu.VMEM((LANE, LANE), F32)],
        compiler_params=_params("parallel", "parallel", "arbitrary"),
        name="rwkv7_chunkwise",
    )(p, p, p, p, p, p, p, p, p, p, p, p,
      mu_pad, mu_pad, mu_pad, mu_pad, mu_pad, mu_pad,
      w0, a0, k_k, k_a, r_k, ln_w, ln_b, w_up, a_up, g_up_pad)


def _row(v, width=None):
    v = v.reshape(1, -1).astype(F32)
    if width is not None and v.shape[1] < width:
        v = jnp.pad(v, ((0, 0), (0, width - v.shape[1])))
    return v


def kernel(x, c, ada_w, ada_b, norm1, w_in, fox_f_bias, fox_norm, ml_conv_w, ml_conv_b, ml_i_bias, ml_f_bias,
           ml_norm, rw_mu, rw_w0, rw_w_up, rw_a0, rw_a_up, rw_g_up, rw_k_k, rw_k_a, rw_r_k, rw_ln_w, rw_ln_b,
           w_out, norm2, ffn_gate, ffn_up, ffn_down, final_norm):
    batch, t, d = x.shape
    depth = ada_w.shape[0]
    assert d == D_MODEL and t % ML_CHUNK == 0 and t % RW_CHUNK == 0

    mod = _modulation(c, ada_w, ada_b)
    mod = mod.reshape(depth, batch, 6, 1, d)
    w_in_t = _repack_w_in(w_in)
    w_out_b = _cast_bf16(w_out, "cast_w_out")
    w_gate_b = _cast_bf16(ffn_gate, "cast_ffn_gate")
    w_up_b = _cast_bf16(ffn_up, "cast_ffn_up")
    w_down_b = _cast_bf16(ffn_down, "cast_ffn_down")

    for l in range(depth):
        sh1, sc1, g1, sh2, sc2, g2 = (mod[l, :, i] for i in range(6))

        h = _norm_mod(x, norm1[l], sc1, sh1, BF16)
        p = _matmul_plain(h, w_in_t, l, F32, tm=1024, tn=512)

        fox_bias_row = _row(fox_f_bias[l], LANE)
        ml_bias_row = _row(jnp.concatenate([ml_i_bias[l], ml_f_bias[l]]), LANE)
        cum, li_all, bcs_all = _gates(p, fox_bias_row, ml_bias_row)
        cum_h = cum[:, :, :FOX_HEADS].transpose(0, 2, 1)
        y_fox = _fox_attention(p, cum_h[..., None], fox_norm[l].reshape(FOX_WIDTH, 1))

        li = li_all[:, :, :ML_HEADS]
        bcs = bcs_all[:, :, ML_HEADS:2 * ML_HEADS]
        gate_cols = jnp.stack([li, bcs], axis=-1).transpose(0, 2, 1, 3)
        gate_rows = gate_cols.transpose(0, 1, 3, 2)
        y_ml = _mlstm(p, gate_rows, gate_cols, ml_conv_w[l], _row(ml_conv_b[l]), _row(ml_norm[l]))

        mu_full = jnp.zeros((1, N_IN), F32).at[:, _O_RW:].set(rw_mu[l][None, :])
        mu_pad = _pad_columns(mu_full)
        g_up_pad = jnp.pad(rw_g_up[l], ((RW_GATE_LORA_PAD - RW_GATE_LORA, 0), (0, 0))).astype(BF16)
        y_rw = _rwkv7(p, mu_pad, _row(rw_w0[l]), _row(rw_a0[l]), _row(rw_k_k[l]), _row(rw_k_a[l]),
                      _row(rw_r_k[l]), _row(rw_ln_w[l]), _row(rw_ln_b[l]),
                      rw_w_up[l].astype(BF16), rw_a_up[l].astype(BF16), g_up_pad)

        x = _matmul_mix(y_fox, y_ml, y_rw, w_out_b, l, x, g1, tm=1024, tn=512)

        h = _norm_mod(x, norm2[l], sc2, sh2, BF16)
        act = _matmul_swiglu(h, w_gate_b, w_up_b, l, tm=1024, tn=512)
        x = _matmul_residual(act, w_down_b, l, x, g2, tm=512, tn=512, tk=D_FF, name="ffn_down")

    zeros = jnp.zeros((batch, 1, d), F32)
    return _norm_mod(x, final_norm, zeros, zeros, F32)
```

```python
import functools

import jax
import jax.numpy as jnp
from jax import lax
from jax.experimental import pallas as pl
from jax.experimental.pallas import tpu as pltpu

F32 = jnp.float32
BF16 = jnp.bfloat16
HIGHEST = lax.Precision.HIGHEST

D_MODEL = 4096
FOX_HEADS, FOX_HEAD_DIM = 12, 128
FOX_WIDTH = FOX_HEADS * FOX_HEAD_DIM
ML_HEADS, ML_QK_DIM, ML_V_DIM = 4, 128, 256
ML_QK_WIDTH = ML_HEADS * ML_QK_DIM
ML_WIDTH = ML_HEADS * ML_V_DIM
ML_CONV = 4
GATE_SOFTCAP = 15.0
RW_HEADS, RW_HEAD_DIM = 24, 64
RW_WIDTH = RW_HEADS * RW_HEAD_DIM
RW_LORA = 128
RW_GATE_LORA = 480
RW_GATE_LORA_PAD = 512
RW_LN_EPS = 64e-5
D_FF = 11008
NORM_EPS = 1e-6
LOG2_E = 1.4426950408889634

LANE = 128
SUBLANE = 8
VMEM_LIMIT = 56 * 1024 * 1024

_O_FOX_Q, _O_FOX_K, _O_FOX_V, _O_FOX_F = 0, 1536, 3072, 4608
_O_ML = 4620
_O_ML_QK, _O_ML_V, _O_ML_I, _O_ML_F, _O_ML_O = _O_ML, _O_ML + 1024, _O_ML + 2048, _O_ML + 2052, _O_ML + 2056
_O_RW = 7700
_O_RW_R, _O_RW_K, _O_RW_V = _O_RW, _O_RW + 1536, _O_RW + 3072
_O_RW_WL, _O_RW_AL, _O_RW_GL = _O_RW + 4608, _O_RW + 4736, _O_RW + 4864
N_IN = 13044
P_FOX_Q, P_FOX_K, P_FOX_V = 0, 1536, 3072
P_ML_V, P_ML_O, P_ML_Q, P_ML_K = 4608, 5632, 6656, 7168
P_RW_R, P_RW_K, P_RW_V = 7680, 9216, 10752
P_RW_GL, P_RW_WL, P_RW_AL = 12288, 12800, 12928
P_FOX_F, P_ML_IF = 13056, 13184
N_PAD = 13312
_SECTIONS = (
    (P_FOX_Q, _O_FOX_Q, 3 * FOX_WIDTH),
    (P_ML_V, _O_ML_V, ML_WIDTH),
    (P_ML_O, _O_ML_O, ML_WIDTH),
    (P_ML_Q, _O_ML_QK, 2 * ML_QK_WIDTH),
    (P_RW_R, _O_RW_R, 3 * RW_WIDTH),
    (P_RW_GL + RW_GATE_LORA_PAD - RW_GATE_LORA, _O_RW_GL, RW_GATE_LORA),
    (P_RW_WL, _O_RW_WL, 2 * RW_LORA),
    (P_FOX_F, _O_FOX_F, FOX_HEADS),
    (P_ML_IF, _O_ML_I, 2 * ML_HEADS),
)


def _pad_columns(w):
    parts, pos = [], 0
    for p_start, o_start, width in _SECTIONS:
        if p_start > pos:
            parts.append(jnp.zeros(w.shape[:-1] + (p_start - pos,), w.dtype))
        parts.append(w[..., o_start:o_start + width])
        pos = p_start + width
    if pos < N_PAD:
        parts.append(jnp.zeros(w.shape[:-1] + (N_PAD - pos,), w.dtype))
    return jnp.concatenate(parts, axis=-1)


def _params(*sem):
    return pltpu.CompilerParams(dimension_semantics=sem, vmem_limit_bytes=VMEM_LIMIT)


def _repack_table():
    src, lo, hi = [0] * (N_PAD // LANE), [0] * (N_PAD // LANE), [0] * (N_PAD // LANE)
    for p_start, o_start, width in _SECTIONS:
        first = p_start // LANE
        last = (p_start + width - 1) // LANE
        for tile in range(first, last + 1):
            base = tile * LANE
            src[tile] = o_start + (base - p_start)
            lo[tile] = max(p_start - base, 0)
            hi[tile] = min(p_start + width - base, LANE)
            assert 0 <= src[tile] and src[tile] + LANE <= N_IN
    return (jnp.asarray(src, jnp.int32), jnp.asarray(lo, jnp.int32), jnp.asarray(hi, jnp.int32))


def _repack_kernel(src_ref, lo_ref, hi_ref, w_ref, o_ref):
    i = pl.program_id(0)
    row = lax.broadcasted_iota(jnp.int32, (LANE, 1), 0)
    valid = (row >= lo_ref[i]) & (row < hi_ref[i])
    for layer in range(o_ref.shape[0]):
        o_ref[layer] = jnp.where(valid, w_ref[:, layer, :], 0.0).astype(o_ref.dtype)


def _repack_w_in(w_in):
    depth, kdim, _ = w_in.shape
    w_t = jnp.transpose(w_in, (2, 0, 1))
    src, lo, hi = _repack_table()
    return pl.pallas_call(
        _repack_kernel,
        grid_spec=pltpu.PrefetchScalarGridSpec(
            num_scalar_prefetch=3,
            grid=(N_PAD // LANE,),
            in_specs=[pl.BlockSpec((pl.Element(LANE), pl.Element(depth), pl.Element(kdim)),
                                   lambda i, src, lo, hi: (src[i], 0, 0))],
            out_specs=pl.BlockSpec((depth, LANE, kdim), lambda i, src, lo, hi: (0, i, 0)),
        ),
        out_shape=jax.ShapeDtypeStruct((depth, N_PAD, kdim), BF16),
        compiler_params=_params("parallel"),
        name="repack_w_in",
    )(src, lo, hi, w_t)


def _log_sigmoid(x):
    return jnp.minimum(x, 0.0) - jnp.log1p(jnp.exp(-jnp.abs(x)))


def _sigmoid(x):
    return 1.0 / (1.0 + jnp.exp(-x))


def _silu(x):
    return x * _sigmoid(x)


def _dot(a, b, precision=None):
    return jnp.dot(a, b, preferred_element_type=F32, precision=precision)


def _dot_nt(a, b, precision=None):
    return lax.dot_general(a, b, (((1,), (1,)), ((), ())), preferred_element_type=F32, precision=precision)


def _dot_tn(a, b, precision=None):
    return lax.dot_general(a, b, (((0,), (0,)), ((), ())), preferred_element_type=F32, precision=precision)


def _mod_kernel(c_ref, w_ref, b_ref, o_ref):
    a = _silu(c_ref[...]).astype(BF16)
    o_ref[0] = _dot(a, w_ref[0].astype(BF16)) + b_ref[0]


def _modulation(c, ada_w, ada_b):
    depth, d, n = ada_w.shape
    batch = c.shape[0]
    rows = -(-batch // SUBLANE) * SUBLANE
    c_pad = jnp.zeros((rows, d), F32).at[:batch].set(c)
    tn = 512
    out = pl.pallas_call(
        _mod_kernel,
        grid=(depth, n // tn),
        in_specs=[
            pl.BlockSpec((rows, d), lambda l, j: (0, 0)),
            pl.BlockSpec((1, d, tn), lambda l, j: (l, 0, j)),
            pl.BlockSpec((1, 1, tn), lambda l, j: (l, 0, j)),
        ],
        out_specs=pl.BlockSpec((1, rows, tn), lambda l, j: (l, 0, j)),
        out_shape=jax.ShapeDtypeStruct((depth, rows, n), F32),
        compiler_params=_params("parallel", "parallel"),
        name="adaln_modulation",
    )(c_pad, ada_w, ada_b.reshape(depth, 1, n))
    return out[:, :batch]


def _norm_kernel(x_ref, g_ref, sc_ref, sh_ref, o_ref):
    x = x_ref[0]
    y = x * lax.rsqrt(jnp.mean(x * x, axis=-1, keepdims=True) + NORM_EPS)
    y = (y * g_ref[...]) * (1.0 + sc_ref[0]) + sh_ref[0]
    o_ref[0] = y.astype(o_ref.dtype)


def _norm_mod(x, g, scale, shift, out_dtype):
    batch, t, d = x.shape
    tm = min(512, t)
    return pl.pallas_call(
        _norm_kernel,
        grid=(batch, t // tm),
        in_specs=[
            pl.BlockSpec((1, tm, d), lambda b, i: (b, i, 0)),
            pl.BlockSpec((1, d), lambda b, i: (0, 0)),
            pl.BlockSpec((1, 1, d), lambda b, i: (b, 0, 0)),
            pl.BlockSpec((1, 1, d), lambda b, i: (b, 0, 0)),
        ],
        out_specs=pl.BlockSpec((1, tm, d), lambda b, i: (b, i, 0)),
        out_shape=jax.ShapeDtypeStruct((batch, t, d), out_dtype),
        compiler_params=_params("parallel", "parallel"),
        name="rmsnorm_modulate",
    )(x, g.reshape(1, d), scale, shift)


def _mm_plain_kernel(a_ref, wt_ref, o_ref):
    o_ref[0] = _dot_nt(a_ref[0], wt_ref[0]).astype(o_ref.dtype)


def _mm_swiglu_kernel(a_ref, wg_ref, wu_ref, o_ref):
    a = a_ref[0]
    gate = _dot(a, wg_ref[0])
    up = _dot(a, wu_ref[0])
    o_ref[0] = (_silu(gate) * up).astype(o_ref.dtype)


def _mm_residual_kernel(a_ref, w_ref, x_ref, g_ref, o_ref, acc_ref, *, nk):
    k = pl.program_id(3)

    @pl.when(k == 0)
    def _():
        acc_ref[...] = jnp.zeros_like(acc_ref)

    acc_ref[...] += _dot(a_ref[0], w_ref[0])

    @pl.when(k == nk - 1)
    def _():
        o_ref[0] = x_ref[0] + g_ref[0] * acc_ref[...]


def _mm_mix_kernel(yf_ref, ym_ref, yr_ref, w_ref, x_ref, g_ref, o_ref):
    k1, k2 = FOX_WIDTH, FOX_WIDTH + ML_WIDTH
    acc = _dot(yf_ref[0], w_ref[0, :k1, :])
    acc += _dot(ym_ref[0], w_ref[0, k1:k2, :])
    acc += _dot(yr_ref[0], w_ref[0, k2:, :])
    o_ref[0] = x_ref[0] + g_ref[0] * acc


def _cast_kernel(w_ref, o_ref):
    o_ref[...] = w_ref[...].astype(o_ref.dtype)


def _cast_bf16(w, name):
    depth, r, c = w.shape
    tr = 256
    return pl.pallas_call(
        _cast_kernel,
        grid=(depth, r // tr),
        in_specs=[pl.BlockSpec((1, tr, c), lambda l, i: (l, i, 0))],
        out_specs=pl.BlockSpec((1, tr, c), lambda l, i: (l, i, 0)),
        out_shape=jax.ShapeDtypeStruct(w.shape, BF16),
        compiler_params=_params("parallel", "parallel"),
        name=name,
    )(w)


def _matmul_plain(a, w_t, layer, out_dtype, tm, tn):
    batch, t, kdim = a.shape
    n = w_t.shape[1]
    tm = min(tm, t)
    return pl.pallas_call(
        _mm_plain_kernel,
        grid=(batch, t // tm, n // tn),
        in_specs=[
            pl.BlockSpec((1, tm, kdim), lambda b, i, j: (b, i, 0)),
            pl.BlockSpec((1, tn, kdim), lambda b, i, j: (layer, j, 0)),
        ],
        out_specs=pl.BlockSpec((1, tm, tn), lambda b, i, j: (b, i, j)),
        out_shape=jax.ShapeDtypeStruct((batch, t, n), out_dtype),
        compiler_params=_params("parallel", "parallel", "arbitrary"),
        name="input_projection",
    )(a, w_t)


def _matmul_swiglu(a, wg, wu, layer, tm, tn):
    batch, t, kdim = a.shape
    n = wg.shape[2]
    tm = min(tm, t)
    w_spec = pl.BlockSpec((1, kdim, tn), lambda b, i, j: (layer, 0, j))
    return pl.pallas_call(
        _mm_swiglu_kernel,
        grid=(batch, t // tm, pl.cdiv(n, tn)),
        in_specs=[pl.BlockSpec((1, tm, kdim), lambda b, i, j: (b, i, 0)), w_spec, w_spec],
        out_specs=pl.BlockSpec((1, tm, tn), lambda b, i, j: (b, i, j)),
        out_shape=jax.ShapeDtypeStruct((batch, t, n), BF16),
        compiler_params=_params("parallel", "parallel", "arbitrary"),
        name="ffn_gate_up_swiglu",
    )(a, wg, wu)


def _matmul_residual(a, w, layer, x, gate, tm, tn, tk, name):
    batch, t, kdim = a.shape
    n = w.shape[2]
    tm = min(tm, t)
    assert kdim % tk == 0 and tk % LANE == 0
    nk = kdim // tk
    return pl.pallas_call(
        functools.partial(_mm_residual_kernel, nk=nk),
        grid=(batch, t // tm, n // tn, nk),
        in_specs=[
            pl.BlockSpec((1, tm, tk), lambda b, i, j, k: (b, i, k)),
            pl.BlockSpec((1, tk, tn), lambda b, i, j, k: (layer, k, j)),
            pl.BlockSpec((1, tm, tn), lambda b, i, j, k: (b, i, j)),
            pl.BlockSpec((1, 1, tn), lambda b, i, j, k: (b, 0, j)),
        ],
        out_specs=pl.BlockSpec((1, tm, tn), lambda b, i, j, k: (b, i, j)),
        out_shape=jax.ShapeDtypeStruct((batch, t, n), F32),
        scratch_shapes=[pltpu.VMEM((tm, tn), F32)],
        compiler_params=_params("parallel", "parallel", "parallel", "arbitrary"),
        name=name,
    )(a, w, x, gate)


def _matmul_mix(y_fox, y_ml, y_rw, w, layer, x, gate, tm, tn):
    batch, t, _ = x.shape
    kdim, n = w.shape[1], w.shape[2]
    tm = min(tm, t)
    act = lambda width: pl.BlockSpec((1, tm, width), lambda b, i, j: (b, i, 0))
    return pl.pallas_call(
        _mm_mix_kernel,
        grid=(batch, t // tm, n // tn),
        in_specs=[
            act(FOX_WIDTH), act(ML_WIDTH), act(RW_WIDTH),
            pl.BlockSpec((1, kdim, tn), lambda b, i, j: (layer, 0, j)),
            pl.BlockSpec((1, tm, tn), lambda b, i, j: (b, i, j)),
            pl.BlockSpec((1, 1, tn), lambda b, i, j: (b, 0, j)),
        ],
        out_specs=pl.BlockSpec((1, tm, tn), lambda b, i, j: (b, i, j)),
        out_shape=jax.ShapeDtypeStruct((batch, t, n), F32),
        compiler_params=_params("parallel", "parallel", "arbitrary"),
        name="output_projection",
    )(y_fox, y_ml, y_rw, w, x, gate)


ML_CHUNK = 128
_CUM_ROWS = 256


def _gates_kernel(ff_ref, mif_ref, fb_ref, mb_ref, cum_ref, li_ref, bcs_ref, *, t):
    r = _CUM_ROWS if t % _CUM_ROWS == 0 else t
    row = lax.broadcasted_iota(jnp.int32, (r, r), 0)
    col = lax.broadcasted_iota(jnp.int32, (r, r), 1)
    tril = (col <= row).astype(F32)
    fb = fb_ref[...]

    def cum_body(i, carry):
        start = pl.multiple_of(i * r, r)
        lg = _log_sigmoid(ff_ref[0, pl.ds(start, r), :] + fb)
        cs = _dot(tril, lg, HIGHEST) + carry
        cum_ref[0, pl.ds(start, r), :] = cs
        return cs[r - 1:r, :]

    lax.fori_loop(0, t // r, cum_body, jnp.zeros((1, LANE), F32))

    lc = ML_CHUNK
    rowc = lax.broadcasted_iota(jnp.int32, (lc, lc), 0)
    colc = lax.broadcasted_iota(jnp.int32, (lc, lc), 1)
    trilc = (colc <= rowc).astype(F32)
    mb = mb_ref[...]

    def ml_body(i, carry):
        start = pl.multiple_of(i * lc, lc)
        z = mif_ref[0, pl.ds(start, lc), :] + mb
        capped = GATE_SOFTCAP * jnp.tanh(z / GATE_SOFTCAP)
        li_ref[0, pl.ds(start, lc), :] = capped
        bcs_ref[0, pl.ds(start, lc), :] = _dot(trilc, _log_sigmoid(capped), HIGHEST)
        return carry

    lax.fori_loop(0, t // lc, ml_body, 0)


def _gates(p, fox_bias_row, ml_bias_row):
    batch, t, _ = p.shape
    blk = lambda col: pl.BlockSpec((1, t, LANE), lambda b: (b, 0, col // LANE))
    row = pl.BlockSpec((1, LANE), lambda b: (0, 0))
    out = pl.BlockSpec((1, t, LANE), lambda b: (b, 0, 0))
    shp = jax.ShapeDtypeStruct((batch, t, LANE), F32)
    return pl.pallas_call(
        functools.partial(_gates_kernel, t=t),
        grid=(batch,),
        in_specs=[blk(P_FOX_F), blk(P_ML_IF), row, row],
        out_specs=[out, out, out],
        out_shape=[shp, shp, shp],
        compiler_params=_params("parallel"),
        name="gate_cumsums",
    )(p, p, fox_bias_row, ml_bias_row)


FOX_TQ = 1024
FOX_TK = 1024


def _fox_kernel(q_ref, k_ref, v_ref, fk_ref, g_ref, o_ref, kb_ref, vt_ref, *, t, tq, tk):
    i = pl.program_id(2)
    dh = FOX_HEAD_DIM

    @pl.when(i == 0)
    def _():
        for c in range(t // tk):
            rows = slice(c * tk, (c + 1) * tk)
            kb_ref[rows, :] = k_ref[0, rows, :].astype(BF16)
            vt_ref[:, rows] = jnp.transpose(v_ref[0, rows, :]).astype(BF16)

    q = (q_ref[0] * (dh ** -0.5 * LOG2_E)).astype(BF16)
    q_pos = i * tq + lax.broadcasted_iota(jnp.int32, (1, tq), 1)

    def step(j, carry, masked):
        m, l, acc = carry
        start = pl.multiple_of(j * tk, tk)
        fk = fk_ref[0, 0, pl.ds(start, tk), :] * LOG2_E
        s = _dot_nt(kb_ref[pl.ds(start, tk), :], q) - fk
        if masked:
            k_pos = start + lax.broadcasted_iota(jnp.int32, (tk, 1), 0)
            s = jnp.where(k_pos <= q_pos, s, -jnp.inf)
        m_new = jnp.maximum(m, jnp.max(s, axis=0, keepdims=True))
        p = jnp.exp2(s - m_new)
        alpha = jnp.exp2(m - m_new)
        l = alpha * l + jnp.sum(p, axis=0, keepdims=True)
        acc = alpha * acc + _dot(vt_ref[:, pl.ds(start, tk)], p.astype(BF16))
        return m_new, l, acc

    per_tile = tq // tk
    carry = (jnp.full((1, tq), -1e30, F32), jnp.zeros((1, tq), F32), jnp.zeros((dh, tq), F32))
    carry = lax.fori_loop(0, i * per_tile, functools.partial(step, masked=False), carry)
    _, l, acc = lax.fori_loop(i * per_tile, (i + 1) * per_tile, functools.partial(step, masked=True), carry)
    o = acc / l
    o = o * lax.rsqrt(jnp.mean(o * o, axis=0, keepdims=True) + NORM_EPS) * g_ref[...]
    o_ref[0] = jnp.transpose(o).astype(o_ref.dtype)


def _fox_attention(p, cum_col, out_g_col):
    batch, t, _ = p.shape
    tq = min(FOX_TQ, t)
    tk = min(FOX_TK, tq)
    dh = FOX_HEAD_DIM
    return pl.pallas_call(
        functools.partial(_fox_kernel, t=t, tq=tq, tk=tk),
        grid=(batch, FOX_HEADS, t // tq),
        in_specs=[
            pl.BlockSpec((1, tq, dh), lambda b, h, i: (b, i, P_FOX_Q // dh + h)),
            pl.BlockSpec((1, t, dh), lambda b, h, i: (b, 0, P_FOX_K // dh + h)),
            pl.BlockSpec((1, t, dh), lambda b, h, i: (b, 0, P_FOX_V // dh + h)),
            pl.BlockSpec((1, 1, t, 1), lambda b, h, i: (b, h, 0, 0)),
            pl.BlockSpec((dh, 1), lambda b, h, i: (h, 0)),
        ],
        out_specs=pl.BlockSpec((1, tq, dh), lambda b, h, i: (b, i, h)),
        out_shape=jax.ShapeDtypeStruct((batch, t, FOX_WIDTH), BF16),
        scratch_shapes=[pltpu.VMEM((t, dh), BF16), pltpu.VMEM((dh, t), BF16)],
        compiler_params=_params("parallel", "parallel", "arbitrary"),
        name="fox_attention",
    )(p, p, p, cum_col, out_g_col)


def _mlstm_kernel(q_ref, k_ref, v_ref, o_ref, grow_ref, gcol_ref, cwq_ref, cwk_ref, cbq_ref, cbk_ref,
                  ng_ref, y_ref, *, t):
    lc = ML_CHUNK
    dk, dv = ML_QK_DIM, ML_V_DIM
    row = lax.broadcasted_iota(jnp.int32, (lc, lc), 0)
    col = lax.broadcasted_iota(jnp.int32, (lc, lc), 1)
    causal = col <= row
    cwq, cwk = cwq_ref[...], cwk_ref[...]
    cbq, cbk = cbq_ref[...], cbk_ref[...]
    norm_g = ng_ref[...]

    def conv_silu(ref, start, prev_ok, w, b):
        cur = ref[0, pl.ds(start, lc), :]
        prev = ref[0, pl.ds(jnp.maximum(start - SUBLANE, 0), SUBLANE), :]
        prev = jnp.where(prev_ok, prev, 0.0)
        ext = jnp.concatenate([prev, cur], axis=0)
        out = b
        for j in range(ML_CONV):
            off = SUBLANE - (ML_CONV - 1) + j
            out = out + ext[off:off + lc, :] * w[j:j + 1, :]
        return _silu(out)

    def body(c, carry):
        cmat, nvec, m = carry
        start = pl.multiple_of(c * lc, lc)
        prev_ok = c > 0
        qt = conv_silu(q_ref, start, prev_ok, cwq, cbq) * (dk ** -0.5)
        kt = conv_silu(k_ref, start, prev_ok, cwk, cbk)
        vt = v_ref[0, pl.ds(start, lc), :]
        li_row = grow_ref[0, 0, 0:1, pl.ds(start, lc)]
        b_row = grow_ref[0, 0, 1:2, pl.ds(start, lc)]
        li_col = gcol_ref[0, 0, pl.ds(start, lc), 0:1]
        b_col = gcol_ref[0, 0, pl.ds(start, lc), 1:2]
        g = b_row[:, lc - 1:lc]

        a_inter = b_col + m
        dm = jnp.where(causal, b_col - b_row + li_row, -jnp.inf)
        m_t = jnp.maximum(a_inter, jnp.max(dm, axis=-1, keepdims=True))
        w_inter = jnp.exp(a_inter - m_t)
        qb, kb = qt.astype(BF16), kt.astype(BF16)
        s = _dot_nt(qb, kb) * jnp.exp(dm - m_t)
        num = w_inter * _dot(qb, cmat.astype(BF16)) + _dot(s.astype(BF16), vt.astype(BF16))
        den = w_inter * jnp.sum(qt * nvec, axis=-1, keepdims=True) + jnp.sum(s, axis=-1, keepdims=True)
        h = num / jnp.maximum(jnp.abs(den), jnp.exp(-m_t))

        upd = g - b_col + li_col
        m_new = jnp.maximum(g + m, jnp.max(upd, axis=0, keepdims=True))
        decay = jnp.exp(g + m - m_new)
        wk = jnp.exp(upd - m_new) * kt
        cmat = decay * cmat + _dot_tn(wk.astype(BF16), vt.astype(BF16))
        nvec = decay * nvec + jnp.sum(wk, axis=0, keepdims=True)

        hn = h * lax.rsqrt(jnp.mean(h * h, axis=-1, keepdims=True) + NORM_EPS) * norm_g
        og = o_ref[0, pl.ds(start, lc), :]
        y_ref[0, pl.ds(start, lc), :] = (hn * _sigmoid(og)).astype(y_ref.dtype)
        return cmat, nvec, m_new

    init = (jnp.zeros((dk, dv), F32), jnp.zeros((1, dk), F32), jnp.zeros((1, 1), F32))
    lax.fori_loop(0, t // lc, body, init)


def _mlstm(p, gate_rows, gate_cols, conv_w, conv_b, norm_g):
    batch, t, _ = p.shape
    dk, dv = ML_QK_DIM, ML_V_DIM
    nh = ML_HEADS
    return pl.pallas_call(
        functools.partial(_mlstm_kernel, t=t),
        grid=(batch, nh),
        in_specs=[
            pl.BlockSpec((1, t, dk), lambda b, h: (b, 0, P_ML_Q // dk + h)),
            pl.BlockSpec((1, t, dk), lambda b, h: (b, 0, P_ML_K // dk + h)),
            pl.BlockSpec((1, t, dv), lambda b, h: (b, 0, P_ML_V // dv + h)),
            pl.BlockSpec((1, t, dv), lambda b, h: (b, 0, P_ML_O // dv + h)),
            pl.BlockSpec((1, 1, 2, t), lambda b, h: (b, h, 0, 0)),
            pl.BlockSpec((1, 1, t, 2), lambda b, h: (b, h, 0, 0)),
            pl.BlockSpec((ML_CONV, dk), lambda b, h: (0, h)),
            pl.BlockSpec((ML_CONV, dk), lambda b, h: (0, nh + h)),
            pl.BlockSpec((1, dk), lambda b, h: (0, h)),
            pl.BlockSpec((1, dk), lambda b, h: (0, nh + h)),
            pl.BlockSpec((1, dv), lambda b, h: (0, h)),
        ],
        out_specs=pl.BlockSpec((1, t, dv), lambda b, h: (b, 0, h)),
        out_shape=jax.ShapeDtypeStruct((batch, t, ML_WIDTH), BF16),
        compiler_params=_params("parallel", "parallel"),
        name="mlstm_chunkwise",
    )(p, p, p, p, gate_rows, gate_cols, conv_w, conv_w, conv_b, conv_b, norm_g)


RW_CHUNK = 64
RW_TILE = 2048


def _rw_kernel(r_ref, k_ref, v_ref, wl_ref, al_ref, gl_ref,
               rp_ref, kp_ref, vp_ref, wlp_ref, alp_ref, glp_ref,
               mur_ref, muk_ref, muv_ref, muw_ref, mua_ref, mug_ref,
               w0_ref, a0_ref, kkw_ref, kaw_ref, rkw_ref, lnw_ref, lnb_ref,
               wup_ref, aup_ref, gup_ref, y_ref, s_ref, *, tm):
    i = pl.program_id(2)
    lc = RW_CHUNK
    n = RW_HEAD_DIM
    pair = LANE

    @pl.when(i == 0)
    def _():
        s_ref[...] = jnp.zeros_like(s_ref)

    first = lax.broadcasted_iota(jnp.int32, (tm, 1), 0) == 0
    has_prev = i > 0

    def shift(cur_ref, prev_ref, mu_ref):
        cur = cur_ref[0]
        last = jnp.where(has_prev, prev_ref[0][SUBLANE - 1:SUBLANE, :], 0.0)
        prev = jnp.where(first, last, pltpu.roll(cur, 1, axis=0))
        return cur + (prev - cur) * mu_ref[...]

    r = shift(r_ref, rp_ref, mur_ref)
    k = shift(k_ref, kp_ref, muk_ref)
    v = shift(v_ref, vp_ref, muv_ref)
    wl = shift(wl_ref, wlp_ref, muw_ref)
    al = shift(al_ref, alp_ref, mua_ref)
    gl = shift(gl_ref, glp_ref, mug_ref)

    hrow = lax.broadcasted_iota(jnp.int32, (pair, pair), 0)
    hcol = lax.broadcasted_iota(jnp.int32, (pair, pair), 1)
    same_head = jnp.where(hrow // n == hcol // n, 1.0, 0.0).astype(BF16)

    def head_sum(x):
        hi = x.astype(BF16)
        lo = (x - hi.astype(F32)).astype(BF16)
        return _dot(hi, same_head) + _dot(lo, same_head)

    z = w0_ref[...] + _dot(jnp.tanh(wl).astype(BF16), wup_ref[...])
    lw = -jnp.exp(_log_sigmoid(z) - 0.5)
    a = _sigmoid(a0_ref[...] + _dot(al.astype(BF16), aup_ref[...]))
    gate = _dot(_sigmoid(gl).astype(BF16), gup_ref[...])
    kk = k * kkw_ref[...]
    kk = kk / jnp.maximum(jnp.sqrt(head_sum(kk * kk)), 1e-12)
    k = k * (1.0 + (a - 1.0) * kaw_ref[...])
    bb = kk * a
    bonus = head_sum(r * k * rkw_ref[...]) * v

    trow = lax.broadcasted_iota(jnp.int32, (lc, lc), 0)
    tcol = lax.broadcasted_iota(jnp.int32, (lc, lc), 1)
    tri = jnp.where(tcol <= trow, 1.0, 0.0).astype(BF16)
    lw_hi = lw.astype(BF16)
    rem = lw - lw_hi.astype(F32)
    lw_mid = rem.astype(BF16)
    lw_lo = (rem - lw_mid.astype(F32)).astype(BF16)
    nchunk = tm // lc
    cums, ends = [], []
    for c in range(nchunk):
        sl = slice(c * lc, (c + 1) * lc)
        cum_c = _dot(tri, lw_hi[sl]) + (_dot(tri, lw_mid[sl]) + _dot(tri, lw_lo[sl]))
        cums.append(cum_c)
        ends.append(jnp.broadcast_to(cum_c[lc - 1:lc, :], (lc, LANE)))
    cum = jnp.concatenate(cums, axis=0)
    cum_end = jnp.concatenate(ends, axis=0)
    e_inc = jnp.exp(cum)
    e_neg = jnp.exp(-cum)
    tail = jnp.exp(cum_end - cum)
    rt = r * e_inc
    at = kk * jnp.exp(cum - lw)
    kt = k * e_neg
    bt = bb * e_neg
    kh = k * tail
    bh = bb * tail

    head0 = lax.broadcasted_iota(jnp.int32, (1, LANE), 1) < n

    def stack(x, dtype=BF16):
        return jnp.concatenate([jnp.where(head0, x, 0.0), jnp.where(head0, 0.0, x)], axis=0).astype(dtype)

    prow = lax.broadcasted_iota(jnp.int32, (pair, pair), 0)
    pcol = lax.broadcasted_iota(jnp.int32, (pair, pair), 1)
    lower_strict = pcol < prow
    lower_incl = pcol <= prow
    eye = jnp.where(pcol == prow, 1.0, 0.0)

    chunks = range(nchunk)
    rows = [slice(c * lc, (c + 1) * lc) for c in chunks]
    a2 = [stack(at[sl]) for sl in rows]
    r2 = [stack(rt[sl], F32) for sl in rows]
    v2 = [stack(v[sl]) for sl in rows]
    mm = [_dot_nt(jnp.concatenate([a2[c], r2[c].astype(BF16)], axis=0),
                  jnp.concatenate([stack(bt[rows[c]]), stack(kt[rows[c]])], axis=0)) for c in chunks]
    m_ak = [jnp.where(lower_strict, mm[c][:pair, pair:], 0.0).astype(BF16) for c in chunks]
    m_rb = [jnp.where(lower_incl, mm[c][pair:, :pair], 0.0).astype(BF16) for c in chunks]
    m_rk = [jnp.where(lower_incl, mm[c][pair:, pair:], 0.0).astype(BF16) for c in chunks]

    m_ab = [jnp.where(lower_strict, mm[c][:pair, :pair], 0.0) for c in chunks]
    i_plus_m = [(eye + m_ab[c]).astype(BF16) for c in chunks]
    t_inv = [eye - m_ab[c] for c in chunks]
    for _ in range((lc - 1).bit_length() - 1):
        res = [_dot(i_plus_m[c], t_inv[c].astype(BF16)) for c in chunks]
        t_inv = [_dot(t_inv[c].astype(BF16), (2.0 * eye - res[c]).astype(BF16)) for c in chunks]

    x1 = [_dot(m_ak[c], v2[c]) for c in chunks]
    aub = [_dot(t_inv[c].astype(BF16), jnp.concatenate([a2[c], x1[c].astype(BF16)], axis=1)).astype(BF16)
           for c in chunks]
    rb = [_dot(m_rb[c], aub[c]) for c in chunks]
    r_eff = [r2[c] - rb[c][:, :pair] for c in chunks]
    y_loc = [_dot(m_rk[c], v2[c]) - rb[c][:, pair:] for c in chunks]
    r_eff = [(r_eff[c][:lc] + r_eff[c][lc:]).astype(BF16) for c in chunks]
    y_loc = [y_loc[c][:lc] + y_loc[c][lc:] for c in chunks]
    gd = [_dot_tn(aub[c], stack(bh[rows[c]])) for c in chunks]
    g_mat = [(eye * e_inc[(c + 1) * lc - 1:(c + 1) * lc, :] - gd[c][:pair]).astype(BF16) for c in chunks]
    d_mat = [_dot_tn(v2[c], stack(kh[rows[c]])) - gd[c][pair:] for c in chunks]

    s_mat = s_ref[...]
    ys = []
    for c in chunks:
        s_b = s_mat.astype(BF16)
        ys.append(_dot_nt(r_eff[c], s_b) + y_loc[c])
        s_mat = _dot(s_b, g_mat[c]) + d_mat[c]
    s_ref[...] = s_mat

    y = jnp.concatenate(ys, axis=0)
    mean = head_sum(y) * (1.0 / n)
    cen = y - mean
    var = head_sum(cen * cen) * (1.0 / n)
    y = cen * lax.rsqrt(var + RW_LN_EPS) * lnw_ref[...] + lnb_ref[...]
    y_ref[0] = ((y + bonus) * gate).astype(y_ref.dtype)


def _rwkv7(p, mu_pad, w0, a0, k_k, k_a, r_k, ln_w, ln_b, w_up, a_up, g_up_pad):
    batch, t, _ = p.shape
    tm = min(RW_TILE, t)
    nj = RW_WIDTH // LANE
    gw = RW_GATE_LORA_PAD
    rb = tm // SUBLANE

    def cur(col, width=LANE, tiled=True):
        base = col // width
        if tiled:
            return pl.BlockSpec((1, tm, width), lambda b, j, i: (b, i, base + j))
        return pl.BlockSpec((1, tm, width), lambda b, j, i: (b, i, base))

    def prev(col, width=LANE, tiled=True):
        base = col // width
        if tiled:
            return pl.BlockSpec((1, SUBLANE, width), lambda b, j, i: (b, jnp.maximum(i * rb - 1, 0), base + j))
        return pl.BlockSpec((1, SUBLANE, width), lambda b, j, i: (b, jnp.maximum(i * rb - 1, 0), base))

    def murow(col, width=LANE, tiled=True):
        base = col // width
        if tiled:
            return pl.BlockSpec((1, width), lambda b, j, i: (0, base + j))
        return pl.BlockSpec((1, width), lambda b, j, i: (0, base))

    vec = pl.BlockSpec((1, LANE), lambda b, j, i: (0, j))
    up = pl.BlockSpec((RW_LORA, LANE), lambda b, j, i: (0, j))
    gup = pl.BlockSpec((gw, LANE), lambda b, j, i: (0, j))
    return pl.pallas_call(
        functools.partial(_rw_kernel, tm=tm),
        grid=(batch, nj, t // tm),
        in_specs=[
            cur(P_RW_R), cur(P_RW_K), cur(P_RW_V),
            cur(P_RW_WL, tiled=False), cur(P_RW_AL, tiled=False), cur(P_RW_GL, gw, tiled=False),
            prev(P_RW_R), prev(P_RW_K), prev(P_RW_V),
            prev(P_RW_WL, tiled=False), prev(P_RW_AL, tiled=False), prev(P_RW_GL, gw, tiled=False),
            murow(P_RW_R), murow(P_RW_K), murow(P_RW_V),
            murow(P_RW_WL, tiled=False), murow(P_RW_AL, tiled=False), murow(P_RW_GL, gw, tiled=False),
            vec, vec, vec, vec, vec, vec, vec, up, up, gup,
        ],
        out_specs=pl.BlockSpec((1, tm, LANE), lambda b, j, i: (b, i, j)),
        out_shape=jax.ShapeDtypeStruct((batch, t, RW_WIDTH), BF16),
        scratch_shapes=[pltpu.VMEM((LANE, LANE), F32)],
        compiler_params=_params("parallel", "parallel", "arbitrary"),
        name="rwkv7_chunkwise",
    )(p, p, p, p, p, p, p, p, p, p, p, p,
      mu_pad, mu_pad, mu_pad, mu_pad, mu_pad, mu_pad,
      w0, a0, k_k, k_a, r_k, ln_w, ln_b, w_up, a_up, g_up_pad)


def _row(v, width=None):
    v = v.reshape(1, -1).astype(F32)
    if width is not None and v.shape[1] < width:
        v = jnp.pad(v, ((0, 0), (0, width - v.shape[1])))
    return v


def kernel(x, c, ada_w, ada_b, norm1, w_in, fox_f_bias, fox_norm, ml_conv_w, ml_conv_b, ml_i_bias, ml_f_bias,
           ml_norm, rw_mu, rw_w0, rw_w_up, rw_a0, rw_a_up, rw_g_up, rw_k_k, rw_k_a, rw_r_k, rw_ln_w, rw_ln_b,
           w_out, norm2, ffn_gate, ffn_up, ffn_down, final_norm):
    batch, t, d = x.shape
    depth = ada_w.shape[0]
    assert d == D_MODEL and t % ML_CHUNK == 0 and t % RW_CHUNK == 0

    mod = _modulation(c, ada_w, ada_b)
    mod = mod.reshape(depth, batch, 6, 1, d)
    w_in_t = _repack_w_in(w_in)
    w_out_b = _cast_bf16(w_out, "cast_w_out")
    w_gate_b = _cast_bf16(ffn_gate, "cast_ffn_gate")
    w_up_b = _cast_bf16(ffn_up, "cast_ffn_up")
    w_down_b = _cast_bf16(ffn_down, "cast_ffn_down")

    for l in range(depth):
        sh1, sc1, g1, sh2, sc2, g2 = (mod[l, :, i] for i in range(6))

        h = _norm_mod(x, norm1[l], sc1, sh1, BF16)
        p = _matmul_plain(h, w_in_t, l, F32, tm=1024, tn=1024)

        fox_bias_row = _row(fox_f_bias[l], LANE)
        ml_bias_row = _row(jnp.concatenate([ml_i_bias[l], ml_f_bias[l]]), LANE)
        cum, li_all, bcs_all = _gates(p, fox_bias_row, ml_bias_row)
        cum_h = cum[:, :, :FOX_HEADS].transpose(0, 2, 1)
        y_fox = _fox_attention(p, cum_h[..., None], fox_norm[l].reshape(FOX_WIDTH, 1))

        li = li_all[:, :, :ML_HEADS]
        bcs = bcs_all[:, :, ML_HEADS:2 * ML_HEADS]
        gate_cols = jnp.stack([li, bcs], axis=-1).transpose(0, 2, 1, 3)
        gate_rows = gate_cols.transpose(0, 1, 3, 2)
        y_ml = _mlstm(p, gate_rows, gate_cols, ml_conv_w[l], _row(ml_conv_b[l]), _row(ml_norm[l]))

        mu_full = jnp.zeros((1, N_IN), F32).at[:, _O_RW:].set(rw_mu[l][None, :])
        mu_pad = _pad_columns(mu_full)
        g_up_pad = jnp.pad(rw_g_up[l], ((RW_GATE_LORA_PAD - RW_GATE_LORA, 0), (0, 0))).astype(BF16)
        y_rw = _rwkv7(p, mu_pad, _row(rw_w0[l]), _row(rw_a0[l]), _row(rw_k_k[l]), _row(rw_k_a[l]),
                      _row(rw_r_k[l]), _row(rw_ln_w[l]), _row(rw_ln_b[l]),
                      rw_w_up[l].astype(BF16), rw_a_up[l].astype(BF16), g_up_pad)

        x = _matmul_mix(y_fox, y_ml, y_rw, w_out_b, l, x, g1, tm=1024, tn=512)

        h = _norm_mod(x, norm2[l], sc2, sh2, BF16)
        act = _matmul_swiglu(h, w_gate_b, w_up_b, l, tm=1024, tn=512)
        x = _matmul_residual(act, w_down_b, l, x, g2, tm=512, tn=512, tk=D_FF, name="ffn_down")

    zeros = jnp.zeros((batch, 1, d), F32)
    return _norm_mod(x, final_norm, zeros, zeros, F32)
```

```python
import functools

import jax
import jax.numpy as jnp
from jax import lax
from jax.experimental import pallas as pl
from jax.experimental.pallas import tpu as pltpu

F32 = jnp.float32
BF16 = jnp.bfloat16
HIGHEST = lax.Precision.HIGHEST

D_MODEL = 4096
FOX_HEADS, FOX_HEAD_DIM = 12, 128
FOX_WIDTH = FOX_HEADS * FOX_HEAD_DIM
ML_HEADS, ML_QK_DIM, ML_V_DIM = 4, 128, 256
ML_QK_WIDTH = ML_HEADS * ML_QK_DIM
ML_WIDTH = ML_HEADS * ML_V_DIM
ML_CONV = 4
GATE_SOFTCAP = 15.0
RW_HEADS, RW_HEAD_DIM = 24, 64
RW_WIDTH = RW_HEADS * RW_HEAD_DIM
RW_LORA = 128
RW_GATE_LORA = 480
RW_GATE_LORA_PAD = 512
RW_LN_EPS = 64e-5
D_FF = 11008
NORM_EPS = 1e-6
LOG2_E = 1.4426950408889634

LANE = 128
SUBLANE = 8
VMEM_LIMIT = 56 * 1024 * 1024

_O_FOX_Q, _O_FOX_K, _O_FOX_V, _O_FOX_F = 0, 1536, 3072, 4608
_O_ML = 4620
_O_ML_QK, _O_ML_V, _O_ML_I, _O_ML_F, _O_ML_O = _O_ML, _O_ML + 1024, _O_ML + 2048, _O_ML + 2052, _O_ML + 2056
_O_RW = 7700
_O_RW_R, _O_RW_K, _O_RW_V = _O_RW, _O_RW + 1536, _O_RW + 3072
_O_RW_WL, _O_RW_AL, _O_RW_GL = _O_RW + 4608, _O_RW + 4736, _O_RW + 4864
N_IN = 13044
P_FOX_Q, P_FOX_K, P_FOX_V = 0, 1536, 3072
P_ML_V, P_ML_O, P_ML_Q, P_ML_K = 4608, 5632, 6656, 7168
P_RW_R, P_RW_K, P_RW_V = 7680, 9216, 10752
P_RW_GL, P_RW_WL, P_RW_AL = 12288, 12800, 12928
P_FOX_F, P_ML_IF = 13056, 13184
N_PAD = 13312
_SECTIONS = (
    (P_FOX_Q, _O_FOX_Q, 3 * FOX_WIDTH),
    (P_ML_V, _O_ML_V, ML_WIDTH),
    (P_ML_O, _O_ML_O, ML_WIDTH),
    (P_ML_Q, _O_ML_QK, 2 * ML_QK_WIDTH),
    (P_RW_R, _O_RW_R, 3 * RW_WIDTH),
    (P_RW_GL + RW_GATE_LORA_PAD - RW_GATE_LORA, _O_RW_GL, RW_GATE_LORA),
    (P_RW_WL, _O_RW_WL, 2 * RW_LORA),
    (P_FOX_F, _O_FOX_F, FOX_HEADS),
    (P_ML_IF, _O_ML_I, 2 * ML_HEADS),
)


def _pad_columns(w):
    parts, pos = [], 0
    for p_start, o_start, width in _SECTIONS:
        if p_start > pos:
            parts.append(jnp.zeros(w.shape[:-1] + (p_start - pos,), w.dtype))
        parts.append(w[..., o_start:o_start + width])
        pos = p_start + width
    if pos < N_PAD:
        parts.append(jnp.zeros(w.shape[:-1] + (N_PAD - pos,), w.dtype))
    return jnp.concatenate(parts, axis=-1)


def _params(*sem):
    return pltpu.CompilerParams(dimension_semantics=sem, vmem_limit_bytes=VMEM_LIMIT)


def _repack_table():
    src, lo, hi = [0] * (N_PAD // LANE), [0] * (N_PAD // LANE), [0] * (N_PAD // LANE)
    for p_start, o_start, width in _SECTIONS:
        first = p_start // LANE
        last = (p_start + width - 1) // LANE
        for tile in range(first, last + 1):
            base = tile * LANE
            src[tile] = o_start + (base - p_start)
            lo[tile] = max(p_start - base, 0)
            hi[tile] = min(p_start + width - base, LANE)
            assert 0 <= src[tile] and src[tile] + LANE <= N_IN
    return (jnp.asarray(src, jnp.int32), jnp.asarray(lo, jnp.int32), jnp.asarray(hi, jnp.int32))


def _repack_kernel(src_ref, lo_ref, hi_ref, w_ref, o_ref):
    i = pl.program_id(0)
    row = lax.broadcasted_iota(jnp.int32, (LANE, 1), 0)
    valid = (row >= lo_ref[i]) & (row < hi_ref[i])
    for layer in range(o_ref.shape[0]):
        o_ref[layer] = jnp.where(valid, w_ref[:, layer, :], 0.0).astype(o_ref.dtype)


def _repack_w_in(w_in):
    depth, kdim, _ = w_in.shape
    w_t = jnp.transpose(w_in, (2, 0, 1))
    src, lo, hi = _repack_table()
    return pl.pallas_call(
        _repack_kernel,
        grid_spec=pltpu.PrefetchScalarGridSpec(
            num_scalar_prefetch=3,
            grid=(N_PAD // LANE,),
            in_specs=[pl.BlockSpec((pl.Element(LANE), pl.Element(depth), pl.Element(kdim)),
                                   lambda i, src, lo, hi: (src[i], 0, 0))],
            out_specs=pl.BlockSpec((depth, LANE, kdim), lambda i, src, lo, hi: (0, i, 0)),
        ),
        out_shape=jax.ShapeDtypeStruct((depth, N_PAD, kdim), BF16),
        compiler_params=_params("parallel"),
        name="repack_w_in",
    )(src, lo, hi, w_t)


def _log_sigmoid(x):
    return jnp.minimum(x, 0.0) - jnp.log1p(jnp.exp(-jnp.abs(x)))


def _sigmoid(x):
    return 1.0 / (1.0 + jnp.exp(-x))


def _silu(x):
    return x * _sigmoid(x)


def _dot(a, b, precision=None):
    return jnp.dot(a, b, preferred_element_type=F32, precision=precision)


def _dot_nt(a, b, precision=None):
    return lax.dot_general(a, b, (((1,), (1,)), ((), ())), preferred_element_type=F32, precision=precision)


def _dot_tn(a, b, precision=None):
    return lax.dot_general(a, b, (((0,), (0,)), ((), ())), preferred_element_type=F32, precision=precision)


def _mod_kernel(c_ref, w_ref, b_ref, o_ref):
    a = _silu(c_ref[...]).astype(BF16)
    o_ref[0] = _dot(a, w_ref[0].astype(BF16)) + b_ref[0]


def _modulation(c, ada_w, ada_b):
    depth, d, n = ada_w.shape
    batch = c.shape[0]
    rows = -(-batch // SUBLANE) * SUBLANE
    c_pad = jnp.zeros((rows, d), F32).at[:batch].set(c)
    tn = 512
    out = pl.pallas_call(
        _mod_kernel,
        grid=(depth, n // tn),
        in_specs=[
            pl.BlockSpec((rows, d), lambda l, j: (0, 0)),
            pl.BlockSpec((1, d, tn), lambda l, j: (l, 0, j)),
            pl.BlockSpec((1, 1, tn), lambda l, j: (l, 0, j)),
        ],
        out_specs=pl.BlockSpec((1, rows, tn), lambda l, j: (l, 0, j)),
        out_shape=jax.ShapeDtypeStruct((depth, rows, n), F32),
        compiler_params=_params("parallel", "parallel"),
        name="adaln_modulation",
    )(c_pad, ada_w, ada_b.reshape(depth, 1, n))
    return out[:, :batch]


def _norm_kernel(x_ref, g_ref, sc_ref, sh_ref, o_ref):
    x = x_ref[0]
    y = x * lax.rsqrt(jnp.mean(x * x, axis=-1, keepdims=True) + NORM_EPS)
    y = (y * g_ref[...]) * (1.0 + sc_ref[0]) + sh_ref[0]
    o_ref[0] = y.astype(o_ref.dtype)


def _norm_mod(x, g, scale, shift, out_dtype):
    batch, t, d = x.shape
    tm = min(512, t)
    return pl.pallas_call(
        _norm_kernel,
        grid=(batch, t // tm),
        in_specs=[
            pl.BlockSpec((1, tm, d), lambda b, i: (b, i, 0)),
            pl.BlockSpec((1, d), lambda b, i: (0, 0)),
            pl.BlockSpec((1, 1, d), lambda b, i: (b, 0, 0)),
            pl.BlockSpec((1, 1, d), lambda b, i: (b, 0, 0)),
        ],
        out_specs=pl.BlockSpec((1, tm, d), lambda b, i: (b, i, 0)),
        out_shape=jax.ShapeDtypeStruct((batch, t, d), out_dtype),
        compiler_params=_params("parallel", "parallel"),
        name="rmsnorm_modulate",
    )(x, g.reshape(1, d), scale, shift)


def _mm_plain_kernel(a_ref, wt_ref, o_ref):
    o_ref[0] = _dot_nt(a_ref[0], wt_ref[0]).astype(o_ref.dtype)


def _mm_swiglu_kernel(a_ref, wg_ref, wu_ref, o_ref):
    a = a_ref[0]
    gate = _dot(a, wg_ref[0])
    up = _dot(a, wu_ref[0])
    o_ref[0] = (_silu(gate) * up).astype(o_ref.dtype)


def _mm_residual_kernel(a_ref, w_ref, x_ref, g_ref, o_ref, acc_ref, *, nk):
    k = pl.program_id(3)

    @pl.when(k == 0)
    def _():
        acc_ref[...] = jnp.zeros_like(acc_ref)

    acc_ref[...] += _dot(a_ref[0], w_ref[0])

    @pl.when(k == nk - 1)
    def _():
        o_ref[0] = x_ref[0] + g_ref[0] * acc_ref[...]


def _mm_mix_kernel(yf_ref, ym_ref, yr_ref, w_ref, x_ref, g_ref, o_ref):
    k1, k2 = FOX_WIDTH, FOX_WIDTH + ML_WIDTH
    acc = _dot(yf_ref[0], w_ref[0, :k1, :])
    acc += _dot(ym_ref[0], w_ref[0, k1:k2, :])
    acc += _dot(yr_ref[0], w_ref[0, k2:, :])
    o_ref[0] = x_ref[0] + g_ref[0] * acc


def _cast_kernel(w_ref, o_ref):
    o_ref[...] = w_ref[...].astype(o_ref.dtype)


def _cast_bf16(w, name):
    depth, r, c = w.shape
    tr = 256
    return pl.pallas_call(
        _cast_kernel,
        grid=(depth, r // tr),
        in_specs=[pl.BlockSpec((1, tr, c), lambda l, i: (l, i, 0))],
        out_specs=pl.BlockSpec((1, tr, c), lambda l, i: (l, i, 0)),
        out_shape=jax.ShapeDtypeStruct(w.shape, BF16),
        compiler_params=_params("parallel", "parallel"),
        name=name,
    )(w)


def _matmul_plain(a, w_t, layer, out_dtype, tm, tn):
    batch, t, kdim = a.shape
    n = w_t.shape[1]
    tm = min(tm, t)
    return pl.pallas_call(
        _mm_plain_kernel,
        grid=(batch, t // tm, n // tn),
        in_specs=[
            pl.BlockSpec((1, tm, kdim), lambda b, i, j: (b, i, 0)),
            pl.BlockSpec((1, tn, kdim), lambda b, i, j: (layer, j, 0)),
        ],
        out_specs=pl.BlockSpec((1, tm, tn), lambda b, i, j: (b, i, j)),
        out_shape=jax.ShapeDtypeStruct((batch, t, n), out_dtype),
        compiler_params=_params("parallel", "parallel", "arbitrary"),
        name="input_projection",
    )(a, w_t)


def _matmul_swiglu(a, wg, wu, layer, tm, tn):
    batch, t, kdim = a.shape
    n = wg.shape[2]
    tm = min(tm, t)
    w_spec = pl.BlockSpec((1, kdim, tn), lambda b, i, j: (layer, 0, j))
    return pl.pallas_call(
        _mm_swiglu_kernel,
        grid=(batch, t // tm, pl.cdiv(n, tn)),
        in_specs=[pl.BlockSpec((1, tm, kdim), lambda b, i, j: (b, i, 0)), w_spec, w_spec],
        out_specs=pl.BlockSpec((1, tm, tn), lambda b, i, j: (b, i, j)),
        out_shape=jax.ShapeDtypeStruct((batch, t, n), BF16),
        compiler_params=_params("parallel", "parallel", "arbitrary"),
        name="ffn_gate_up_swiglu",
    )(a, wg, wu)


def _matmul_residual(a, w, layer, x, gate, tm, tn, tk, name):
    batch, t, kdim = a.shape
    n = w.shape[2]
    tm = min(tm, t)
    assert kdim % tk == 0 and tk % LANE == 0
    nk = kdim // tk
    return pl.pallas_call(
        functools.partial(_mm_residual_kernel, nk=nk),
        grid=(batch, t // tm, n // tn, nk),
        in_specs=[
            pl.BlockSpec((1, tm, tk), lambda b, i, j, k: (b, i, k)),
            pl.BlockSpec((1, tk, tn), lambda b, i, j, k: (layer, k, j)),
            pl.BlockSpec((1, tm, tn), lambda b, i, j, k: (b, i, j)),
            pl.BlockSpec((1, 1, tn), lambda b, i, j, k: (b, 0, j)),
        ],
        out_specs=pl.BlockSpec((1, tm, tn), lambda b, i, j, k: (b, i, j)),
        out_shape=jax.ShapeDtypeStruct((batch, t, n), F32),
        scratch_shapes=[pltpu.VMEM((tm, tn), F32)],
        compiler_params=_params("parallel", "parallel", "parallel", "arbitrary"),
        name=name,
    )(a, w, x, gate)


def _matmul_mix(y_fox, y_ml, y_rw, w, layer, x, gate, tm, tn):
    batch, t, _ = x.shape
    kdim, n = w.shape[1], w.shape[2]
    tm = min(tm, t)
    act = lambda width: pl.BlockSpec((1, tm, width), lambda b, i, j: (b, i, 0))
    return pl.pallas_call(
        _mm_mix_kernel,
        grid=(batch, t // tm, n // tn),
        in_specs=[
            act(FOX_WIDTH), act(ML_WIDTH), act(RW_WIDTH),
            pl.BlockSpec((1, kdim, tn), lambda b, i, j: (layer, 0, j)),
            pl.BlockSpec((1, tm, tn), lambda b, i, j: (b, i, j)),
            pl.BlockSpec((1, 1, tn), lambda b, i, j: (b, 0, j)),
        ],
        out_specs=pl.BlockSpec((1, tm, tn), lambda b, i, j: (b, i, j)),
        out_shape=jax.ShapeDtypeStruct((batch, t, n), F32),
        compiler_params=_params("parallel", "parallel", "arbitrary"),
        name="output_projection",
    )(y_fox, y_ml, y_rw, w, x, gate)


ML_CHUNK = 256
_CUM_ROWS = 256


def _gates_kernel(ff_ref, mif_ref, fb_ref, mb_ref, cum_ref, li_ref, bcs_ref, *, t):
    r = _CUM_ROWS if t % _CUM_ROWS == 0 else t
    row = lax.broadcasted_iota(jnp.int32, (r, r), 0)
    col = lax.broadcasted_iota(jnp.int32, (r, r), 1)
    tril = (col <= row).astype(F32)
    fb = fb_ref[...]

    def cum_body(i, carry):
        start = pl.multiple_of(i * r, r)
        lg = _log_sigmoid(ff_ref[0, pl.ds(start, r), :] + fb)
        cs = _dot(tril, lg, HIGHEST) + carry
        cum_ref[0, pl.ds(start, r), :] = cs
        return cs[r - 1:r, :]

    lax.fori_loop(0, t // r, cum_body, jnp.zeros((1, LANE), F32))

    lc = ML_CHUNK
    rowc = lax.broadcasted_iota(jnp.int32, (lc, lc), 0)
    colc = lax.broadcasted_iota(jnp.int32, (lc, lc), 1)
    trilc = (colc <= rowc).astype(F32)
    mb = mb_ref[...]

    def ml_body(i, carry):
        start = pl.multiple_of(i * lc, lc)
        z = mif_ref[0, pl.ds(start, lc), :] + mb
        capped = GATE_SOFTCAP * jnp.tanh(z / GATE_SOFTCAP)
        li_ref[0, pl.ds(start, lc), :] = capped
        bcs_ref[0, pl.ds(start, lc), :] = _dot(trilc, _log_sigmoid(capped), HIGHEST)
        return carry

    lax.fori_loop(0, t // lc, ml_body, 0)


def _gates(p, fox_bias_row, ml_bias_row):
    batch, t, _ = p.shape
    blk = lambda col: pl.BlockSpec((1, t, LANE), lambda b: (b, 0, col // LANE))
    row = pl.BlockSpec((1, LANE), lambda b: (0, 0))
    out = pl.BlockSpec((1, t, LANE), lambda b: (b, 0, 0))
    shp = jax.ShapeDtypeStruct((batch, t, LANE), F32)
    return pl.pallas_call(
        functools.partial(_gates_kernel, t=t),
        grid=(batch,),
        in_specs=[blk(P_FOX_F), blk(P_ML_IF), row, row],
        out_specs=[out, out, out],
        out_shape=[shp, shp, shp],
        compiler_params=_params("parallel"),
        name="gate_cumsums",
    )(p, p, fox_bias_row, ml_bias_row)


FOX_TQ = 1024
FOX_TK = 1024


def _fox_kernel(q_ref, k_ref, v_ref, fk_ref, g_ref, o_ref, kb_ref, vt_ref, *, t, tq, tk):
    i = pl.program_id(2)
    dh = FOX_HEAD_DIM

    @pl.when(i == 0)
    def _():
        for c in range(t // tk):
            rows = slice(c * tk, (c + 1) * tk)
            kb_ref[rows, :] = k_ref[0, rows, :].astype(BF16)
            vt_ref[:, rows] = jnp.transpose(v_ref[0, rows, :]).astype(BF16)

    q = (q_ref[0] * (dh ** -0.5 * LOG2_E)).astype(BF16)
    q_pos = i * tq + lax.broadcasted_iota(jnp.int32, (1, tq), 1)

    def step(j, carry, masked):
        m, l, acc = carry
        start = pl.multiple_of(j * tk, tk)
        fk = fk_ref[0, 0, pl.ds(start, tk), :] * LOG2_E
        s = _dot_nt(kb_ref[pl.ds(start, tk), :], q) - fk
        if masked:
            k_pos = start + lax.broadcasted_iota(jnp.int32, (tk, 1), 0)
            s = jnp.where(k_pos <= q_pos, s, -jnp.inf)
        m_new = jnp.maximum(m, jnp.max(s, axis=0, keepdims=True))
        p = jnp.exp2(s - m_new)
        alpha = jnp.exp2(m - m_new)
        l = alpha * l + jnp.sum(p, axis=0, keepdims=True)
        acc = alpha * acc + _dot(vt_ref[:, pl.ds(start, tk)], p.astype(BF16))
        return m_new, l, acc

    per_tile = tq // tk
    carry = (jnp.full((1, tq), -1e30, F32), jnp.zeros((1, tq), F32), jnp.zeros((dh, tq), F32))
    carry = lax.fori_loop(0, i * per_tile, functools.partial(step, masked=False), carry)
    _, l, acc = lax.fori_loop(i * per_tile, (i + 1) * per_tile, functools.partial(step, masked=True), carry)
    o = acc / l
    o = o * lax.rsqrt(jnp.mean(o * o, axis=0, keepdims=True) + NORM_EPS) * g_ref[...]
    o_ref[0] = jnp.transpose(o).astype(o_ref.dtype)


def _fox_attention(p, cum_col, out_g_col):
    batch, t, _ = p.shape
    tq = min(FOX_TQ, t)
    tk = min(FOX_TK, tq)
    dh = FOX_HEAD_DIM
    return pl.pallas_call(
        functools.partial(_fox_kernel, t=t, tq=tq, tk=tk),
        grid=(batch, FOX_HEADS, t // tq),
        in_specs=[
            pl.BlockSpec((1, tq, dh), lambda b, h, i: (b, i, P_FOX_Q // dh + h)),
            pl.BlockSpec((1, t, dh), lambda b, h, i: (b, 0, P_FOX_K // dh + h)),
            pl.BlockSpec((1, t, dh), lambda b, h, i: (b, 0, P_FOX_V // dh + h)),
            pl.BlockSpec((1, 1, t, 1), lambda b, h, i: (b, h, 0, 0)),
            pl.BlockSpec((dh, 1), lambda b, h, i: (h, 0)),
        ],
        out_specs=pl.BlockSpec((1, tq, dh), lambda b, h, i: (b, i, h)),
        out_shape=jax.ShapeDtypeStruct((batch, t, FOX_WIDTH), BF16),
        scratch_shapes=[pltpu.VMEM((t, dh), BF16), pltpu.VMEM((dh, t), BF16)],
        compiler_params=_params("parallel", "parallel", "arbitrary"),
        name="fox_attention",
    )(p, p, p, cum_col, out_g_col)


def _mlstm_kernel(q_ref, k_ref, v_ref, o_ref, grow_ref, gcol_ref, cwq_ref, cwk_ref, cbq_ref, cbk_ref,
                  ng_ref, y_ref, *, t):
    lc = ML_CHUNK
    dk, dv = ML_QK_DIM, ML_V_DIM
    row = lax.broadcasted_iota(jnp.int32, (lc, lc), 0)
    col = lax.broadcasted_iota(jnp.int32, (lc, lc), 1)
    causal = col <= row
    cwq, cwk = cwq_ref[...], cwk_ref[...]
    cbq, cbk = cbq_ref[...], cbk_ref[...]
    norm_g = ng_ref[...]

    def conv_silu(ref, start, prev_ok, w, b):
        cur = ref[0, pl.ds(start, lc), :]
        prev = ref[0, pl.ds(jnp.maximum(start - SUBLANE, 0), SUBLANE), :]
        prev = jnp.where(prev_ok, prev, 0.0)
        ext = jnp.concatenate([prev, cur], axis=0)
        out = b
        for j in range(ML_CONV):
            off = SUBLANE - (ML_CONV - 1) + j
            out = out + ext[off:off + lc, :] * w[j:j + 1, :]
        return _silu(out)

    def body(c, carry):
        cmat, nvec, m = carry
        start = pl.multiple_of(c * lc, lc)
        prev_ok = c > 0
        qt = conv_silu(q_ref, start, prev_ok, cwq, cbq) * (dk ** -0.5)
        kt = conv_silu(k_ref, start, prev_ok, cwk, cbk)
        vt = v_ref[0, pl.ds(start, lc), :]
        li_row = grow_ref[0, 0, 0:1, pl.ds(start, lc)]
        b_row = grow_ref[0, 0, 1:2, pl.ds(start, lc)]
        li_col = gcol_ref[0, 0, pl.ds(start, lc), 0:1]
        b_col = gcol_ref[0, 0, pl.ds(start, lc), 1:2]
        g = b_row[:, lc - 1:lc]

        a_inter = b_col + m
        dm = jnp.where(causal, b_col - b_row + li_row, -jnp.inf)
        m_t = jnp.maximum(a_inter, jnp.max(dm, axis=-1, keepdims=True))
        w_inter = jnp.exp(a_inter - m_t)
        qb, kb = qt.astype(BF16), kt.astype(BF16)
        s = _dot_nt(qb, kb) * jnp.exp(dm - m_t)
        num = w_inter * _dot(qb, cmat.astype(BF16)) + _dot(s.astype(BF16), vt.astype(BF16))
        den = w_inter * jnp.sum(qt * nvec, axis=-1, keepdims=True) + jnp.sum(s, axis=-1, keepdims=True)
        h = num / jnp.maximum(jnp.abs(den), jnp.exp(-m_t))

        upd = g - b_col + li_col
        m_new = jnp.maximum(g + m, jnp.max(upd, axis=0, keepdims=True))
        decay = jnp.exp(g + m - m_new)
        wk = jnp.exp(upd - m_new) * kt
        cmat = decay * cmat + _dot_tn(wk.astype(BF16), vt.astype(BF16))
        nvec = decay * nvec + jnp.sum(wk, axis=0, keepdims=True)

        hn = h * lax.rsqrt(jnp.mean(h * h, axis=-1, keepdims=True) + NORM_EPS) * norm_g
        og = o_ref[0, pl.ds(start, lc), :]
        y_ref[0, pl.ds(start, lc), :] = (hn * _sigmoid(og)).astype(y_ref.dtype)
        return cmat, nvec, m_new

    init = (jnp.zeros((dk, dv), F32), jnp.zeros((1, dk), F32), jnp.zeros((1, 1), F32))
    lax.fori_loop(0, t // lc, body, init)


def _mlstm(p, gate_rows, gate_cols, conv_w, conv_b, norm_g):
    batch, t, _ = p.shape
    dk, dv = ML_QK_DIM, ML_V_DIM
    nh = ML_HEADS
    return pl.pallas_call(
        functools.partial(_mlstm_kernel, t=t),
        grid=(batch, nh),
        in_specs=[
            pl.BlockSpec((1, t, dk), lambda b, h: (b, 0, P_ML_Q // dk + h)),
            pl.BlockSpec((1, t, dk), lambda b, h: (b, 0, P_ML_K // dk + h)),
            pl.BlockSpec((1, t, dv), lambda b, h: (b, 0, P_ML_V // dv + h)),
            pl.BlockSpec((1, t, dv), lambda b, h: (b, 0, P_ML_O // dv + h)),
            pl.BlockSpec((1, 1, 2, t), lambda b, h: (b, h, 0, 0)),
            pl.BlockSpec((1, 1, t, 2), lambda b, h: (b, h, 0, 0)),
            pl.BlockSpec((ML_CONV, dk), lambda b, h: (0, h)),
            pl.BlockSpec((ML_CONV, dk), lambda b, h: (0, nh + h)),
            pl.BlockSpec((1, dk), lambda b, h: (0, h)),
            pl.BlockSpec((1, dk), lambda b, h: (0, nh + h)),
            pl.BlockSpec((1, dv), lambda b, h: (0, h)),
        ],
        out_specs=pl.BlockSpec((1, t, dv), lambda b, h: (b, 0, h)),
        out_shape=jax.ShapeDtypeStruct((batch, t, ML_WIDTH), BF16),
        compiler_params=_params("parallel", "parallel"),
        name="mlstm_chunkwise",
    )(p, p, p, p, gate_rows, gate_cols, conv_w, conv_w, conv_b, conv_b, norm_g)


RW_CHUNK = 64
RW_TILE = 2048


def _rw_kernel(r_ref, k_ref, v_ref, wl_ref, al_ref, gl_ref,
               rp_ref, kp_ref, vp_ref, wlp_ref, alp_ref, glp_ref,
               mur_ref, muk_ref, muv_ref, muw_ref, mua_ref, mug_ref,
               w0_ref, a0_ref, kkw_ref, kaw_ref, rkw_ref, lnw_ref, lnb_ref,
               wup_ref, aup_ref, gup_ref, y_ref, s_ref, *, tm):
    i = pl.program_id(2)
    lc = RW_CHUNK
    n = RW_HEAD_DIM
    pair = LANE

    @pl.when(i == 0)
    def _():
        s_ref[...] = jnp.zeros_like(s_ref)

    first = lax.broadcasted_iota(jnp.int32, (tm, 1), 0) == 0
    has_prev = i > 0

    def shift(cur_ref, prev_ref, mu_ref):
        cur = cur_ref[0]
        last = jnp.where(has_prev, prev_ref[0][SUBLANE - 1:SUBLANE, :], 0.0)
        prev = jnp.where(first, last, pltpu.roll(cur, 1, axis=0))
        return cur + (prev - cur) * mu_ref[...]

    r = shift(r_ref, rp_ref, mur_ref)
    k = shift(k_ref, kp_ref, muk_ref)
    v = shift(v_ref, vp_ref, muv_ref)
    wl = shift(wl_ref, wlp_ref, muw_ref)
    al = shift(al_ref, alp_ref, mua_ref)
    gl = shift(gl_ref, glp_ref, mug_ref)

    hrow = lax.broadcasted_iota(jnp.int32, (pair, pair), 0)
    hcol = lax.broadcasted_iota(jnp.int32, (pair, pair), 1)
    same_head = jnp.where(hrow // n == hcol // n, 1.0, 0.0).astype(BF16)

    def head_sum(x):
        hi = x.astype(BF16)
        lo = (x - hi.astype(F32)).astype(BF16)
        return _dot(hi, same_head) + _dot(lo, same_head)

    z = w0_ref[...] + _dot(jnp.tanh(wl).astype(BF16), wup_ref[...])
    lw = -jnp.exp(_log_sigmoid(z) - 0.5)
    a = _sigmoid(a0_ref[...] + _dot(al.astype(BF16), aup_ref[...]))
    gate = _dot(_sigmoid(gl).astype(BF16), gup_ref[...])
    kk = k * kkw_ref[...]
    kk = kk / jnp.maximum(jnp.sqrt(head_sum(kk * kk)), 1e-12)
    k = k * (1.0 + (a - 1.0) * kaw_ref[...])
    bb = kk * a
    bonus = head_sum(r * k * rkw_ref[...]) * v

    trow = lax.broadcasted_iota(jnp.int32, (lc, lc), 0)
    tcol = lax.broadcasted_iota(jnp.int32, (lc, lc), 1)
    tri = jnp.where(tcol <= trow, 1.0, 0.0).astype(BF16)
    lw_hi = lw.astype(BF16)
    rem = lw - lw_hi.astype(F32)
    lw_mid = rem.astype(BF16)
    lw_lo = (rem - lw_mid.astype(F32)).astype(BF16)
    nchunk = tm // lc
    cums, ends = [], []
    for c in range(nchunk):
        sl = slice(c * lc, (c + 1) * lc)
        cum_c = _dot(tri, lw_hi[sl]) + (_dot(tri, lw_mid[sl]) + _dot(tri, lw_lo[sl]))
        cums.append(cum_c)
        ends.append(jnp.broadcast_to(cum_c[lc - 1:lc, :], (lc, LANE)))
    cum = jnp.concatenate(cums, axis=0)
    cum_end = jnp.concatenate(ends, axis=0)
    e_inc = jnp.exp(cum)
    e_neg = jnp.exp(-cum)
    tail = jnp.exp(cum_end - cum)
    rt = r * e_inc
    at = kk * jnp.exp(cum - lw)
    kt = k * e_neg
    bt = bb * e_neg
    kh = k * tail
    bh = bb * tail

    head0 = lax.broadcasted_iota(jnp.int32, (1, LANE), 1) < n

    def stack(x, dtype=BF16):
        return jnp.concatenate([jnp.where(head0, x, 0.0), jnp.where(head0, 0.0, x)], axis=0).astype(dtype)

    prow = lax.broadcasted_iota(jnp.int32, (pair, pair), 0)
    pcol = lax.broadcasted_iota(jnp.int32, (pair, pair), 1)
    lower_strict = pcol < prow
    lower_incl = pcol <= prow
    eye = jnp.where(pcol == prow, 1.0, 0.0)

    chunks = range(nchunk)
    rows = [slice(c * lc, (c + 1) * lc) for c in chunks]
    a2 = [stack(at[sl]) for sl in rows]
    r2 = [stack(rt[sl], F32) for sl in rows]
    v2 = [stack(v[sl]) for sl in rows]
    mm = [_dot_nt(jnp.concatenate([a2[c], r2[c].astype(BF16)], axis=0),
                  jnp.concatenate([stack(bt[rows[c]]), stack(kt[rows[c]])], axis=0)) for c in chunks]
    m_ak = [jnp.where(lower_strict, mm[c][:pair, pair:], 0.0).astype(BF16) for c in chunks]
    m_rb = [jnp.where(lower_incl, mm[c][pair:, :pair], 0.0).astype(BF16) for c in chunks]
    m_rk = [jnp.where(lower_incl, mm[c][pair:, pair:], 0.0).astype(BF16) for c in chunks]

    m_ab = [jnp.where(lower_strict, mm[c][:pair, :pair], 0.0) for c in chunks]
    i_plus_m = [(eye + m_ab[c]).astype(BF16) for c in chunks]
    t_inv = [eye - m_ab[c] for c in chunks]
    for _ in range((lc - 1).bit_length() - 1):
        res = [_dot(i_plus_m[c], t_inv[c].astype(BF16)) for c in chunks]
        t_inv = [_dot(t_inv[c].astype(BF16), (2.0 * eye - res[c]).astype(BF16)) for c in chunks]

    x1 = [_dot(m_ak[c], v2[c]) for c in chunks]
    aub = [_dot(t_inv[c].astype(BF16), jnp.concatenate([a2[c], x1[c].astype(BF16)], axis=1)).astype(BF16)
           for c in chunks]
    rb = [_dot(m_rb[c], aub[c]) for c in chunks]
    r_eff = [r2[c] - rb[c][:, :pair] for c in chunks]
    y_loc = [_dot(m_rk[c], v2[c]) - rb[c][:, pair:] for c in chunks]
    r_eff = [(r_eff[c][:lc] + r_eff[c][lc:]).astype(BF16) for c in chunks]
    y_loc = [y_loc[c][:lc] + y_loc[c][lc:] for c in chunks]
    gd = [_dot_tn(aub[c], stack(bh[rows[c]])) for c in chunks]
    g_mat = [(eye * e_inc[(c + 1) * lc - 1:(c + 1) * lc, :] - gd[c][:pair]).astype(BF16) for c in chunks]
    d_mat = [_dot_tn(v2[c], stack(kh[rows[c]])) - gd[c][pair:] for c in chunks]

    s_mat = s_ref[...]
    ys = []
    for c in chunks:
        s_b = s_mat.astype(BF16)
        ys.append(_dot_nt(r_eff[c], s_b) + y_loc[c])
        s_mat = _dot(s_b, g_mat[c]) + d_mat[c]
    s_ref[...] = s_mat

    y = jnp.concatenate(ys, axis=0)
    mean = head_sum(y) * (1.0 / n)
    cen = y - mean
    var = head_sum(cen * cen) * (1.0 / n)
    y = cen * lax.rsqrt(var + RW_LN_EPS) * lnw_ref[...] + lnb_ref[...]
    y_ref[0] = ((y + bonus) * gate).astype(y_ref.dtype)


def _rwkv7(p, mu_pad, w0, a0, k_k, k_a, r_k, ln_w, ln_b, w_up, a_up, g_up_pad):
    batch, t, _ = p.shape
    tm = min(RW_TILE, t)
    nj = RW_WIDTH // LANE
    gw = RW_GATE_LORA_PAD
    rb = tm // SUBLANE

    def cur(col, width=LANE, tiled=True):
        base = col // width
        if tiled:
            return pl.BlockSpec((1, tm, width), lambda b, j, i: (b, i, base + j))
        return pl.BlockSpec((1, tm, width), lambda b, j, i: (b, i, base))

    def prev(col, width=LANE, tiled=True):
        base = col // width
        if tiled:
            return pl.BlockSpec((1, SUBLANE, width), lambda b, j, i: (b, jnp.maximum(i * rb - 1, 0), base + j))
        return pl.BlockSpec((1, SUBLANE, width), lambda b, j, i: (b, jnp.maximum(i * rb - 1, 0), base))

    def murow(col, width=LANE, tiled=True):
        base = col // width
        if tiled:
            return pl.BlockSpec((1, width), lambda b, j, i: (0, base + j))
        return pl.BlockSpec((1, width), lambda b, j, i: (0, base))

    vec = pl.BlockSpec((1, LANE), lambda b, j, i: (0, j))
    up = pl.BlockSpec((RW_LORA, LANE), lambda b, j, i: (0, j))
    gup = pl.BlockSpec((gw, LANE), lambda b, j, i: (0, j))
    return pl.pallas_call(
        functools.partial(_rw_kernel, tm=tm),
        grid=(batch, nj, t // tm),
        in_specs=[
            cur(P_RW_R), cur(P_RW_K), cur(P_RW_V),
            cur(P_RW_WL, tiled=False), cur(P_RW_AL, tiled=False), cur(P_RW_GL, gw, tiled=False),
            prev(P_RW_R), prev(P_RW_K), prev(P_RW_V),
            prev(P_RW_WL, tiled=False), prev(P_RW_AL, tiled=False), prev(P_RW_GL, gw, tiled=False),
            murow(P_RW_R), murow(P_RW_K), murow(P_RW_V),
            murow(P_RW_WL, tiled=False), murow(P_RW_AL, tiled=False), murow(P_RW_GL, gw, tiled=False),
            vec, vec, vec, vec, vec, vec, vec, up, up, gup,
        ],
        out_specs=pl.BlockSpec((1, tm, LANE), lambda b, j, i: (b, i, j)),
        out_shape=jax.ShapeDtypeStruct((batch, t, RW_WIDTH), BF16),
        scratch_shapes=[pltpu.VMEM((LANE, LANE), F32)],
        compiler_params=_params("parallel", "parallel", "arbitrary"),
        name="rwkv7_chunkwise",
    )(p, p, p, p, p, p, p, p, p, p, p, p,
      mu_pad, mu_pad, mu_pad, mu_pad, mu_pad, mu_pad,
      w0, a0, k_k, k_a, r_k, ln_w, ln_b, w_up, a_up, g_up_pad)


def _row(v, width=None):
    v = v.reshape(1, -1).astype(F32)
    if width is not None and v.shape[1] < width:
        v = jnp.pad(v, ((0, 0), (0, width - v.shape[1])))
    return v


def kernel(x, c, ada_w, ada_b, norm1, w_in, fox_f_bias, fox_norm, ml_conv_w, ml_conv_b, ml_i_bias, ml_f_bias,
           ml_norm, rw_mu, rw_w0, rw_w_up, rw_a0, rw_a_up, rw_g_up, rw_k_k, rw_k_a, rw_r_k, rw_ln_w, rw_ln_b,
           w_out, norm2, ffn_gate, ffn_up, ffn_down, final_norm):
    batch, t, d = x.shape
    depth = ada_w.shape[0]
    assert d == D_MODEL and t % ML_CHUNK == 0 and t % RW_CHUNK == 0

    mod = _modulation(c, ada_w, ada_b)
    mod = mod.reshape(depth, batch, 6, 1, d)
    w_in_t = _repack_w_in(w_in)
    w_out_b = _cast_bf16(w_out, "cast_w_out")
    w_gate_b = _cast_bf16(ffn_gate, "cast_ffn_gate")
    w_up_b = _cast_bf16(ffn_up, "cast_ffn_up")
    w_down_b = _cast_bf16(ffn_down, "cast_ffn_down")

    for l in range(depth):
        sh1, sc1, g1, sh2, sc2, g2 = (mod[l, :, i] for i in range(6))

        h = _norm_mod(x, norm1[l], sc1, sh1, BF16)
        p = _matmul_plain(h, w_in_t, l, F32, tm=1024, tn=1024)

        fox_bias_row = _row(fox_f_bias[l], LANE)
        ml_bias_row = _row(jnp.concatenate([ml_i_bias[l], ml_f_bias[l]]), LANE)
        cum, li_all, bcs_all = _gates(p, fox_bias_row, ml_bias_row)
        cum_h = cum[:, :, :FOX_HEADS].transpose(0, 2, 1)
        y_fox = _fox_attention(p, cum_h[..., None], fox_norm[l].reshape(FOX_WIDTH, 1))

        li = li_all[:, :, :ML_HEADS]
        bcs = bcs_all[:, :, ML_HEADS:2 * ML_HEADS]
        gate_cols = jnp.stack([li, bcs], axis=-1).transpose(0, 2, 1, 3)
        gate_rows = gate_cols.transpose(0, 1, 3, 2)
        y_ml = _mlstm(p, gate_rows, gate_cols, ml_conv_w[l], _row(ml_conv_b[l]), _row(ml_norm[l]))

        mu_full = jnp.zeros((1, N_IN), F32).at[:, _O_RW:].set(rw_mu[l][None, :])
        mu_pad = _pad_columns(mu_full)
        g_up_pad = jnp.pad(rw_g_up[l], ((RW_GATE_LORA_PAD - RW_GATE_LORA, 0), (0, 0))).astype(BF16)
        y_rw = _rwkv7(p, mu_pad, _row(rw_w0[l]), _row(rw_a0[l]), _row(rw_k_k[l]), _row(rw_k_a[l]),
                      _row(rw_r_k[l]), _row(rw_ln_w[l]), _row(rw_ln_b[l]),
                      rw_w_up[l].astype(BF16), rw_a_up[l].astype(BF16), g_up_pad)

        x = _matmul_mix(y_fox, y_ml, y_rw, w_out_b, l, x, g1, tm=1024, tn=512)

        h = _norm_mod(x, norm2[l], sc2, sh2, BF16)
        act = _matmul_swiglu(h, w_gate_b, w_up_b, l, tm=1024, tn=512)
        x = _matmul_residual(act, w_down_b, l, x, g2, tm=512, tn=512, tk=D_FF, name="ffn_down")

    zeros = jnp.zeros((batch, 1, d), F32)
    return _norm_mod(x, final_norm, zeros, zeros, F32)
```
